```python
import jax, jax.numpy as jnp
from jax import lax
import numpy as np

D_MODEL = 1024
BATCH = 4
SEQ = 8192
DEPTH = 4

A_HEADS = 4
A_HEAD_DIM = 64
A_WIDTH = A_HEADS * A_HEAD_DIM
B_HEADS = 4
B_HEAD_DIM = 128
B_WIDTH = B_HEADS * B_HEAD_DIM
C_BLOCKS = 4
C_WIDTH = 256
C_BLOCK_DIM = C_WIDTH // C_BLOCKS
D_MIX = A_WIDTH + B_WIDTH + C_WIDTH
IN_SPLITS = (A_WIDTH, A_WIDTH, A_WIDTH, A_WIDTH,
             B_WIDTH, B_WIDTH, B_WIDTH, B_WIDTH, B_HEADS, B_HEADS,
             C_WIDTH, C_WIDTH)
D_IN = 4 * A_WIDTH + 4 * B_WIDTH + 2 * B_HEADS + 2 * C_WIDTH
CONV_K = 4
CHUNK = 64
RG_C = 8.0
D_FF = 4 * D_MODEL
EPS = 1e-6
TINY = 1e-30

kernel_name = "hymba_style_hgrn2_gdn_rglru_hybrid"


def rmsnorm(x, g):
    xf = x.astype(jnp.float32)
    y = xf * lax.rsqrt(jnp.mean(xf * xf, axis=-1, keepdims=True) + EPS)
    return (y * g.astype(jnp.float32)).astype(x.dtype)


def l2norm(x):
    return x * lax.rsqrt(jnp.sum(x * x, axis=-1, keepdims=True) + EPS)


def split_cols(t, sizes):
    offsets = np.cumsum(np.array(sizes))[:-1].tolist()
    return jnp.split(t, offsets, axis=-1)


def causal_depthwise_conv(x, w):
    k = w.shape[0]
    s = x.shape[1]
    xp = jnp.pad(x, ((0, 0), (k - 1, 0), (0, 0)))
    y = xp[:, 0:s] * w[0]
    for j in range(1, k):
        y = y + xp[:, j:j + s] * w[j]
    return y


def masked_exp(mask, t):
    return jnp.where(mask, jnp.exp(jnp.where(mask, t, 0.0)), 0.0)


def to_chunks(t):
    b, s = t.shape[:2]
    t = t.reshape(b, s // CHUNK, CHUNK, *t.shape[2:])
    t = jnp.moveaxis(t, 2, 3)
    return jnp.moveaxis(t, 1, 0)


def from_chunks(t):
    n, b, h, c, d = t.shape
    return jnp.transpose(t, (1, 0, 3, 2, 4)).reshape(b, n * c, h, d)


def hgrn2_chunk(q, k, v, log_f):
    bsz, _, h, dk = q.shape
    dv = v.shape[-1]
    qc, kc, vc = to_chunks(q), to_chunks(k), to_chunks(v)
    cum = jnp.cumsum(to_chunks(log_f), axis=-2)
    cum_last = cum[..., -1:, :]
    q_dec = qc * jnp.exp(cum)
    k_dec = kc * jnp.exp(cum_last - cum)
    chunk_dec = jnp.exp(cum_last[..., 0, :])
    causal = jnp.tril(jnp.ones((CHUNK, CHUNK), dtype=bool))[:, :, None]

    def step(state, xs):
        q_, k_, v_, c_, qd, kd, cd = xs
        diff = c_[..., :, None, :] - c_[..., None, :, :]
        dec = masked_exp(causal, diff)
        attn = jnp.einsum('bhtd,bhsd,bhtsd->bhts', q_, k_, dec)
        o = jnp.einsum('bhtd,bhde->bhte', qd, state) + jnp.einsum('bhts,bhse->bhte', attn, v_)
        state = state * cd[..., :, None] + jnp.einsum('bhsd,bhse->bhde', kd, v_)
        return state, o

    s0 = jnp.zeros((bsz, h, dk, dv), jnp.float32)
    _, o = lax.scan(step, s0, (qc, kc, vc, cum, q_dec, k_dec, chunk_dec))
    return from_chunks(o)


def hgrn2_mixer(q_in, f_in, i_in, g_in, lb, norm_g):
    bsz, s, _ = q_in.shape
    shp = (bsz, s, A_HEADS, A_HEAD_DIM)
    q = jax.nn.silu(q_in.astype(jnp.float32)).reshape(shp)
    fp = f_in.astype(jnp.float32).reshape(shp)
    lbh = lb.astype(jnp.float32).reshape(A_HEADS, A_HEAD_DIM)
    f = lbh + (1.0 - lbh) * jax.nn.sigmoid(fp)
    log_f = jnp.log(jnp.maximum(f, TINY))
    k = (1.0 - lbh) * jax.nn.sigmoid(-fp)
    v = i_in.astype(jnp.float32).reshape(shp)
    o = hgrn2_chunk(q, k, v, log_f)
    o = rmsnorm(o, norm_g) * jax.nn.silu(g_in.astype(jnp.float32).reshape(shp))
    return o.reshape(bsz, s, A_WIDTH)


def gated_delta_chunk(q, k, v, log_alpha, beta):
    bsz, _, h, dk = q.shape
    dv = v.shape[-1]
    qc = to_chunks(q) * (dk ** -0.5)
    kc, vc = to_chunks(k), to_chunks(v)
    gc = jnp.cumsum(to_chunks(log_alpha), axis=-1)
    bc = to_chunks(beta)
    causal = jnp.tril(jnp.ones((CHUNK, CHUNK), dtype=bool))
    strict = jnp.tril(jnp.ones((CHUNK, CHUNK), dtype=bool), -1)
    decay = masked_exp(causal, gc[..., :, None] - gc[..., None, :])
    kb = kc * bc[..., None]
    a_mat = jnp.where(strict, jnp.einsum('nbhtd,nbhsd->nbhts', kb, kc) * decay, 0.0)
    u = lax.linalg.triangular_solve(a_mat, vc * bc[..., None], left_side=True, lower=True,
                                    unit_diagonal=True)
    w = lax.linalg.triangular_solve(a_mat, kb * jnp.exp(gc)[..., None], left_side=True,
                                    lower=True, unit_diagonal=True)
    qk = jnp.einsum('nbhtd,nbhsd->nbhts', qc, kc) * decay
    q_dec = qc * jnp.exp(gc)[..., None]
    k_dec = kc * jnp.exp(gc[..., -1:] - gc)[..., None]
    chunk_dec = jnp.exp(gc[..., -1])

    def step(state, xs):
        qd, kd, qk_, u_, w_, cd = xs
        v_new = u_ - jnp.einsum('bhtd,bhde->bhte', w_, state)
        o = jnp.einsum('bhtd,bhde->bhte', qd, state) + jnp.einsum('bhts,bhse->bhte', qk_, v_new)
        state = state * cd[..., None, None] + jnp.einsum('bhsd,bhse->bhde', kd, v_new)
        return state, o

    s0 = jnp.zeros((bsz, h, dk, dv), jnp.float32)
    _, o = lax.scan(step, s0, (q_dec, k_dec, qk, u, w, chunk_dec))
    return from_chunks(o)


def gdn_mixer(q_in, k_in, v_in, z_in, b_in, a_in, conv_w, a_log, dt_bias, norm_g):
    bsz, s, _ = q_in.shape
    shp = (bsz, s, B_HEADS, B_HEAD_DIM)
    qkv = jnp.concatenate([q_in, k_in, v_in], axis=-1).astype(jnp.float32)
    qkv = jax.nn.silu(causal_depthwise_conv(qkv, conv_w.astype(jnp.float32)))
    q, k, v = split_cols(qkv, (B_WIDTH, B_WIDTH, B_WIDTH))
    q = l2norm(q.reshape(shp))
    k = l2norm(k.reshape(shp))
    v = v.reshape(shp)
    beta = jax.nn.sigmoid(b_in.astype(jnp.float32))
    log_alpha = -jnp.exp(a_log.astype(jnp.float32)) * jax.nn.softplus(
        a_in.astype(jnp.float32) + dt_bias.astype(jnp.float32))
    o = gated_delta_chunk(q, k, v, log_alpha, beta)
    o = rmsnorm(o, norm_g) * jax.nn.silu(z_in.astype(jnp.float32).reshape(shp))
    return o.reshape(bsz, s, B_WIDTH)


def rglru(x, w_a, b_a, w_x, b_x, lam):
    bsz, s, _ = x.shape
    xb = x.reshape(bsz, s, C_BLOCKS, C_BLOCK_DIM)
    r = jax.nn.sigmoid(jnp.einsum('bsnd,nde->bsne', xb, w_a.astype(jnp.float32)).reshape(bsz, s, C_WIDTH)
                       + b_a.astype(jnp.float32))
    i = jax.nn.sigmoid(jnp.einsum('bsnd,nde->bsne', xb, w_x.astype(jnp.float32)).reshape(bsz, s, C_WIDTH)
                       + b_x.astype(jnp.float32))
    log_a = -RG_C * r * jax.nn.softplus(-lam.astype(jnp.float32))
    a = jnp.exp(log_a)
    pos = jnp.arange(s)[None, :, None]
    mult = jnp.where(pos == 0, 1.0, jnp.sqrt(jnp.maximum(-jnp.expm1(2.0 * log_a), EPS)))
    bx = mult * i * x

    def combine(c1, c2):
        a1, b1 = c1
        a2, b2 = c2
        return a1 * a2, a2 * b1 + b2

    _, hs = lax.associative_scan(combine, (a, bx), axis=1)
    return hs


def rglru_mixer(x_in, y_in, conv_w, conv_b, w_a, b_a, w_x, b_x, lam):
    xc = causal_depthwise_conv(x_in.astype(jnp.float32), conv_w.astype(jnp.float32)) + conv_b.astype(jnp.float32)
    h = rglru(xc, w_a, b_a, w_x, b_x, lam)
    return jax.nn.gelu(y_in.astype(jnp.float32), approximate=True) * h


def setup_inputs(seed: int = 0) -> dict:
    key = jax.random.key(seed)
    ks = jax.random.split(key, 24)
    f32 = jnp.float32
    nrm = jax.random.normal
    uni = jax.random.uniform
    x = nrm(ks[0], (BATCH, SEQ, D_MODEL), f32)
    norm1_g = 1.0 + 0.02 * nrm(ks[1], (DEPTH, D_MODEL), f32)
    w_in = nrm(ks[2], (DEPTH, D_MODEL, D_IN), f32) * D_MODEL ** -0.5
    hgrn_lb_logits = 0.1 * nrm(ks[3], (DEPTH, A_WIDTH), f32)
    hgrn_norm_g = 1.0 + 0.02 * nrm(ks[4], (DEPTH, A_HEAD_DIM), f32)
    gdn_conv_w = nrm(ks[5], (DEPTH, CONV_K, 3 * B_WIDTH), f32) * CONV_K ** -0.5
    gdn_a_log = jnp.log(uni(ks[6], (DEPTH, B_HEADS), f32, 1.0, 16.0))
    dt = jnp.exp(uni(ks[7], (DEPTH, B_HEADS), f32, float(np.log(1e-3)), float(np.log(1e-1))))
    gdn_dt_bias = dt + jnp.log(-jnp.expm1(-dt))
    gdn_norm_g = 1.0 + 0.02 * nrm(ks[8], (DEPTH, B_HEAD_DIM), f32)
    lru_conv_w = nrm(ks[9], (DEPTH, CONV_K, C_WIDTH), f32) * CONV_K ** -0.5
    lru_conv_b = 0.01 * nrm(ks[10], (DEPTH, C_WIDTH), f32)
    lru_w_a = nrm(ks[11], (DEPTH, C_BLOCKS, C_BLOCK_DIM, C_BLOCK_DIM), f32) * C_BLOCK_DIM ** -0.5
    lru_b_a = 0.01 * nrm(ks[12], (DEPTH, C_WIDTH), f32)
    lru_w_x = nrm(ks[13], (DEPTH, C_BLOCKS, C_BLOCK_DIM, C_BLOCK_DIM), f32) * C_BLOCK_DIM ** -0.5
    lru_b_x = 0.01 * nrm(ks[14], (DEPTH, C_WIDTH), f32)
    a_c = uni(ks[15], (DEPTH, C_WIDTH), f32, 0.9, 0.999)
    s_a = a_c ** (1.0 / RG_C)
    lru_lambda = jnp.log(s_a) - jnp.log1p(-s_a)
    w_out = nrm(ks[16], (DEPTH, D_MIX, D_MODEL), f32) * D_MIX ** -0.5
    norm2_g = 1.0 + 0.02 * nrm(ks[17], (DEPTH, D_MODEL), f32)
    w_up = nrm(ks[18], (DEPTH, D_MODEL, D_FF), f32) * D_MODEL ** -0.5
    w_down = nrm(ks[19], (DEPTH, D_FF, D_MODEL), f32) * D_FF ** -0.5
    final_norm_g = 1.0 + 0.02 * nrm(ks[20], (D_MODEL,), f32)
    return {"x": x, "norm1_g": norm1_g, "w_in": w_in, "hgrn_lb_logits": hgrn_lb_logits,
            "hgrn_norm_g": hgrn_norm_g, "gdn_conv_w": gdn_conv_w, "gdn_a_log": gdn_a_log,
            "gdn_dt_bias": gdn_dt_bias, "gdn_norm_g": gdn_norm_g, "lru_conv_w": lru_conv_w,
            "lru_conv_b": lru_conv_b, "lru_w_a": lru_w_a, "lru_b_a": lru_b_a, "lru_w_x": lru_w_x,
            "lru_b_x": lru_b_x, "lru_lambda": lru_lambda, "w_out": w_out, "norm2_g": norm2_g,
            "w_up": w_up, "w_down": w_down, "final_norm_g": final_norm_g}


def reference(x, norm1_g, w_in, hgrn_lb_logits, hgrn_norm_g, gdn_conv_w, gdn_a_log, gdn_dt_bias,
              gdn_norm_g, lru_conv_w, lru_conv_b, lru_w_a, lru_b_a, lru_w_x, lru_b_x, lru_lambda,
              w_out, norm2_g, w_up, w_down, final_norm_g):
    p = jax.nn.softmax(hgrn_lb_logits.astype(jnp.float32), axis=0)
    lower_bounds = jnp.clip(jnp.cumsum(p, axis=0) - p[0], 0.0, 1.0 - EPS)
    for l in range(DEPTH):
        h = rmsnorm(x, norm1_g[l])
        proj = h @ w_in[l]
        (aq, af, ai, ag, bq, bk, bv, bz, bb, ba, cx, cy) = split_cols(proj, IN_SPLITS)
        o_a = hgrn2_mixer(aq, af, ai, ag, lower_bounds[l], hgrn_norm_g[l])
        o_b = gdn_mixer(bq, bk, bv, bz, bb, ba, gdn_conv_w[l], gdn_a_log[l], gdn_dt_bias[l], gdn_norm_g[l])
        o_c = rglru_mixer(cx, cy, lru_conv_w[l], lru_conv_b[l], lru_w_a[l], lru_b_a[l],
                          lru_w_x[l], lru_b_x[l], lru_lambda[l])
        mixed = jnp.concatenate([o_a, o_b, o_c], axis=-1).astype(x.dtype)
        x = x + mixed @ w_out[l]
        h2 = rmsnorm(x, norm2_g[l])
        x = x + jnp.square(jax.nn.relu(h2 @ w_up[l])) @ w_down[l]
    return rmsnorm(x, final_norm_g)
```

```python
import functools

import jax
import jax.numpy as jnp
import numpy as np
from jax import lax
from jax.experimental import pallas as pl
from jax.experimental.pallas import tpu as pltpu

F32 = jnp.float32
BF16 = jnp.bfloat16

A_HEADS = 4
A_HEAD_DIM = 64
A_WIDTH = A_HEADS * A_HEAD_DIM
B_HEADS = 4
B_HEAD_DIM = 128
B_WIDTH = B_HEADS * B_HEAD_DIM
C_BLOCKS = 4
C_WIDTH = 256
C_BLOCK_DIM = C_WIDTH // C_BLOCKS
CONV_K = 4
CHUNK = 64
SUB = 16
RG_C = 8.0
EPS = 1e-6
TINY = 1e-30

LANES = 128
SUBLANES = 8
VMEM_LIMIT = 56 * 1024 * 1024

PROJ_TILE = 512
MIX_TILE = 256
FF_CHUNK = 1024

PA_W = 4 * A_WIDTH
PB_W = 4 * B_WIDTH
PC_W = 2 * C_WIDTH
PBA_W = LANES


def _sigmoid(x):
    return 1.0 / (1.0 + jnp.exp(-x))


def _silu(x):
    return x * _sigmoid(x)


def _softplus(x):
    return jnp.maximum(x, 0.0) + jnp.log1p(jnp.exp(-jnp.abs(x)))


def _split3(x):
    hi = x.astype(BF16)
    r1 = x - hi.astype(F32)
    mid = r1.astype(BF16)
    lo = (r1 - mid.astype(F32)).astype(BF16)
    return hi, mid, lo


def _dot(a, b):
    return jnp.dot(a, b, preferred_element_type=F32)


def _dot_nt(a, b):
    return lax.dot_general(a, b, (((1,), (1,)), ((), ())), preferred_element_type=F32)


def _dot_tn(a, b):
    return lax.dot_general(a, b, (((0,), (0,)), ((), ())), preferred_element_type=F32)


def _mask_dot(m01, x):
    hi, mid, lo = _split3(x)
    return _dot(m01, hi) + _dot(m01, mid) + _dot(m01, lo)


def _dot_mask(x, m01):
    hi, mid, lo = _split3(x)
    return _dot(hi, m01) + _dot(mid, m01) + _dot(lo, m01)


def _rms_scale(x):
    return x * lax.rsqrt(jnp.mean(x * x, axis=-1, keepdims=True) + EPS)


def _inproj_kernel(x_ref, g_ref, w_ref, pa_ref, pb_ref, pc_ref, pba_ref):
    h = (_rms_scale(x_ref[...]) * g_ref[...]).astype(BF16)
    off = 0
    for ref, width in ((pa_ref, PA_W), (pb_ref, PB_W), (pc_ref, PC_W), (pba_ref, PBA_W)):
        ref[...] = _dot(h, w_ref[:, off:off + width])
        off += width


def _inproj(xf, g, w):
    t, d = xf.shape
    n = w.shape[1]
    tm = min(PROJ_TILE, t)
    row = lambda i: (i, 0)
    const = lambda i: (0, 0)
    return pl.pallas_call(
        _inproj_kernel,
        grid=(t // tm,),
        in_specs=[pl.BlockSpec((tm, d), row), pl.BlockSpec((1, d), const),
                  pl.BlockSpec((d, n), const)],
        out_specs=[pl.BlockSpec((tm, PA_W), row), pl.BlockSpec((tm, PB_W), row),
                   pl.BlockSpec((tm, PC_W), row), pl.BlockSpec((tm, PBA_W), row)],
        out_shape=[jax.ShapeDtypeStruct((t, PA_W), F32), jax.ShapeDtypeStruct((t, PB_W), F32),
                   jax.ShapeDtypeStruct((t, PC_W), F32), jax.ShapeDtypeStruct((t, PBA_W), F32)],
        compiler_params=pltpu.CompilerParams(dimension_semantics=("arbitrary",),
                                             vmem_limit_bytes=VMEM_LIMIT),
        name="inproj",
    )(xf, g, w)


def _outmlp_kernel(x_ref, oa_ref, ob_ref, oc_ref, wo_ref, g2_ref, wup_ref, wdn_ref, gf_ref,
                   o_ref, *, final):
    x1 = x_ref[...]
    x1 = x1 + _dot(oa_ref[...], wo_ref[0:A_WIDTH, :])
    x1 = x1 + _dot(ob_ref[...], wo_ref[A_WIDTH:A_WIDTH + B_WIDTH, :])
    x1 = x1 + _dot(oc_ref[...], wo_ref[A_WIDTH + B_WIDTH:, :])
    o_ref[...] = x1
    h2 = (_rms_scale(x1) * g2_ref[...]).astype(BF16)
    d_ff = wup_ref.shape[1]
    for c in range(d_ff // FF_CHUNK):
        u = jnp.maximum(_dot(h2, wup_ref[:, c * FF_CHUNK:(c + 1) * FF_CHUNK]), 0.0)
        o_ref[...] += _dot((u * u).astype(BF16), wdn_ref[c * FF_CHUNK:(c + 1) * FF_CHUNK, :])
    if final:
        o_ref[...] = _rms_scale(o_ref[...]) * gf_ref[...]


def _outmlp(xf, oa, ob, oc, wo, g2, wup, wdn, gf, final):
    t, d = xf.shape
    tm = min(PROJ_TILE, t)
    row = lambda i: (i, 0)
    const = lambda i: (0, 0)
    full = lambda a: pl.BlockSpec(a.shape, const)
    return pl.pallas_call(
        functools.partial(_outmlp_kernel, final=final),
        grid=(t // tm,),
        in_specs=[pl.BlockSpec((tm, d), row), pl.BlockSpec((tm, A_WIDTH), row),
                  pl.BlockSpec((tm, B_WIDTH), row), pl.BlockSpec((tm, C_WIDTH), row),
                  full(wo), full(g2), full(wup), full(wdn), full(gf)],
        out_specs=pl.BlockSpec((tm, d), row),
        out_shape=jax.ShapeDtypeStruct((t, d), F32),
        compiler_params=pltpu.CompilerParams(dimension_semantics=("arbitrary",),
                                             vmem_limit_bytes=VMEM_LIMIT),
        name="outmlp",
    )(xf, oa, ob, oc, wo, g2, wup, wdn, gf)


def _hgrn2_kernel(pa_ref, lbl_ref, g_ref, ones_ref, tri_ref, o_ref, st_ref, *, layer):
    tt = pa_ref.shape[0]
    w = A_WIDTH
    nsub = tt // SUB

    @pl.when(pl.program_id(1) == 0)
    def _():
        st_ref[...] = jnp.zeros_like(st_ref)

    logits = lbl_ref[...]
    ex = jnp.exp(logits - jnp.max(logits, axis=0, keepdims=True))
    p = ex / jnp.sum(ex, axis=0, keepdims=True)
    lb = jnp.clip(jnp.sum(p[0:layer + 1], axis=0, keepdims=True) - p[0:1], 0.0, 1.0 - EPS)

    q = _silu(pa_ref[:, 0:w])
    fp = pa_ref[:, w:2 * w]
    v = pa_ref[:, 2 * w:3 * w]
    gate = pa_ref[:, 3 * w:4 * w]
    f = lb + (1.0 - lb) * _sigmoid(fp)
    logf = jnp.log(jnp.maximum(f, TINY))
    k = (1.0 - lb) * _sigmoid(-fp)

    ones_bd = ones_ref[...]
    cum = _mask_dot(tri_ref[...], logf)

    q3 = q.reshape(nsub, SUB, w)
    k3 = k.reshape(nsub, SUB, w)
    v3 = v.reshape(nsub, SUB, w)
    c3 = cum.reshape(nsub, SUB, w)
    row = lax.broadcasted_iota(jnp.int32, (nsub, SUB, w), 1)

    o_intra = jnp.zeros((nsub, SUB, w), F32)
    for j in range(SUB):
        keep = row >= j
        dec = jnp.where(keep, jnp.exp(jnp.where(keep, c3 - c3[:, j:j + 1, :], 0.0)), 0.0)
        x = (q3 * k3[:, j:j + 1, :] * dec).reshape(tt, w).astype(BF16)
        attn = _dot(x, ones_bd)
        o_intra = o_intra + attn.reshape(nsub, SUB, w) * v3[:, j:j + 1, :]

    c_last = c3[:, SUB - 1:SUB, :]
    qd = (q3 * jnp.exp(c3)).astype(BF16)
    kd = (k3 * jnp.exp(c_last - c3)).astype(BF16)
    blk = (lax.broadcasted_iota(jnp.int32, (w, w), 0) // A_HEAD_DIM
           == lax.broadcasted_iota(jnp.int32, (w, w), 1) // A_HEAD_DIM)
    st = st_ref[...]
    o_parts = []
    for i in range(nsub):
        o_parts.append(_dot_nt(qd[i], st.astype(BF16)) + o_intra[i])
        upd = _dot_tn(v3[i].astype(BF16), kd[i])
        st = st * jnp.exp(c_last[i]) + jnp.where(blk, upd, 0.0)
    st_ref[...] = st
    o = jnp.concatenate(o_parts, axis=0)

    ms = _dot_mask(o * o, ones_bd) * (1.0 / A_HEAD_DIM)
    o_ref[...] = (o * lax.rsqrt(ms + EPS) * g_ref[...] * _silu(gate)).astype(o_ref.dtype)


def _hgrn2(pa, lb_logits, norm_g, layer, batch):
    t = pa.shape[0]
    s = t // batch
    tt = min(MIX_TILE, s)
    nt = s // tt
    w = A_WIDTH
    idx = np.arange(w)
    ones_bd = jnp.asarray(idx[:, None] // A_HEAD_DIM == idx[None, :] // A_HEAD_DIM, BF16)
    it = np.arange(tt)
    tri = jnp.asarray((it[:, None] // SUB == it[None, :] // SUB) & (it[:, None] >= it[None, :]), BF16)
    g = jnp.tile(norm_g.astype(F32), A_HEADS).reshape(1, w)
    row = lambda b, i: (b * nt + i, 0)
    const = lambda b, i: (0, 0)
    return pl.pallas_call(
        functools.partial(_hgrn2_kernel, layer=layer),
        grid=(batch, nt),
        in_specs=[pl.BlockSpec((tt, PA_W), row), pl.BlockSpec(lb_logits.shape, const),
                  pl.BlockSpec((1, w), const), pl.BlockSpec((w, w), const),
                  pl.BlockSpec((tt, tt), const)],
        out_specs=pl.BlockSpec((tt, w), row),
        out_shape=jax.ShapeDtypeStruct((t, w), BF16),
        scratch_shapes=[pltpu.VMEM((w, w), F32)],
        compiler_params=pltpu.CompilerParams(dimension_semantics=("arbitrary", "arbitrary"),
                                             vmem_limit_bytes=VMEM_LIMIT),
        name="hgrn2",
    )(pa, lb_logits.astype(F32), g, ones_bd, tri)


def _gdn_kernel(pb_ref, pba_ref, cw_ref, alog_ref, dtb_ref, g_ref, tri_ref, o_ref,
                xbuf_ref, s_ref):
    tt = pb_ref.shape[0]
    hd = B_HEAD_DIM
    nchunk = tt // CHUNK
    qkv_w = 3 * B_WIDTH
    pad = SUBLANES

    @pl.when(pl.program_id(1) == 0)
    def _():
        s_ref[...] = jnp.zeros_like(s_ref)
        xbuf_ref[0:pad, :] = jnp.zeros((pad, qkv_w), F32)

    xbuf_ref[pad:pad + tt, :] = pb_ref[:, 0:qkv_w]
    y = xbuf_ref[pad - CONV_K + 1:pad - CONV_K + 1 + tt, :] * cw_ref[0:1, :]
    for j in range(1, CONV_K):
        y = y + xbuf_ref[pad - CONV_K + 1 + j:pad - CONV_K + 1 + j + tt, :] * cw_ref[j:j + 1, :]
    xbuf_ref[0:pad, :] = xbuf_ref[tt:tt + pad, :]
    qkv = _silu(y)

    lane = lax.broadcasted_iota(jnp.int32, (1, LANES), 1)
    is_alpha = (lane >= B_HEADS) & (lane < 2 * B_HEADS)
    ba = pba_ref[...]
    beta_all = _sigmoid(ba)
    rate = jnp.where(is_alpha, jnp.exp(alog_ref[...]), 0.0)
    log_alpha = -rate * _softplus(ba + dtb_ref[...])
    gc_all = _mask_dot(tri_ref[...], log_alpha)
    gc_t = gc_all.T

    r_i = lax.broadcasted_iota(jnp.int32, (CHUNK, CHUNK), 0)
    c_i = lax.broadcasted_iota(jnp.int32, (CHUNK, CHUNK), 1)
    causal = r_i >= c_i
    strict = r_i > c_i
    eye = (r_i == c_i).astype(F32)
    scale = hd ** -0.5

    for h in range(B_HEADS):
        qh = qkv[:, h * hd:(h + 1) * hd]
        kh = qkv[:, B_WIDTH + h * hd:B_WIDTH + (h + 1) * hd]
        vh = qkv[:, 2 * B_WIDTH + h * hd:2 * B_WIDTH + (h + 1) * hd]
        qh = qh * lax.rsqrt(jnp.sum(qh * qh, axis=-1, keepdims=True) + EPS)
        kh = kh * lax.rsqrt(jnp.sum(kh * kh, axis=-1, keepdims=True) + EPS)
        beta = jnp.broadcast_to(beta_all[:, h:h + 1], (tt, hd))
        g = jnp.broadcast_to(gc_all[:, B_HEADS + h:B_HEADS + h + 1], (tt, hd))
        g_row = gc_t[B_HEADS + h:B_HEADS + h + 1, :]
        eg = jnp.exp(g)
        kb = kh * beta
        vb = vh * beta
        s = s_ref[h]
        for c in range(nchunk):
            r0 = c * CHUNK
            sl = slice(r0, r0 + CHUNK)
            diff = g[sl, 0:CHUNK] - g_row[:, sl]
            decay = jnp.where(causal, jnp.exp(jnp.where(causal, diff, 0.0)), 0.0)
            k_bf = kh[sl].astype(BF16)
            kk = _dot_nt(kb[sl].astype(BF16), k_bf)
            qk = _dot_nt(qh[sl].astype(BF16), k_bf) * scale * decay
            a_mat = jnp.where(strict, kk * decay, 0.0)
            t_inv = eye - jnp.where(r_i // 2 == c_i // 2, a_mat, 0.0)
            b = 2
            while b < CHUNK:
                off = jnp.where((r_i // (2 * b) == c_i // (2 * b)) & (r_i // b != c_i // b), a_mat, 0.0)
                t_bf = t_inv.astype(BF16)
                t_inv = t_inv - _dot(t_bf, _dot(off.astype(BF16), t_bf).astype(BF16))
                b *= 2
            rhs = jnp.concatenate([vb[sl], kb[sl] * eg[sl]], axis=1).astype(BF16)
            uw = _dot(t_inv.astype(BF16), rhs)
            u = uw[:, 0:hd]
            w = uw[:, hd:2 * hd]
            g_last = g[r0 + CHUNK - 1:r0 + CHUNK, :]
            qd = qh[sl] * (scale * eg[sl])
            kd = kh[sl] * jnp.exp(g_last - g[sl])
            m1 = _dot(jnp.concatenate([w, qd], axis=0).astype(BF16), s.astype(BF16))
            v_new = u - m1[0:CHUNK]
            m2 = _dot(jnp.concatenate([qk, kd.T], axis=0).astype(BF16), v_new.astype(BF16))
            o = m1[CHUNK:2 * CHUNK] + m2[0:CHUNK]
            s = s * jnp.exp(g_last) + m2[CHUNK:CHUNK + hd]
            z = pb_ref[sl, qkv_w + h * hd:qkv_w + (h + 1) * hd]
            o_ref[sl, h * hd:(h + 1) * hd] = (_rms_scale(o) * g_ref[...] * _silu(z)).astype(o_ref.dtype)
        s_ref[h] = s


def _gdn(pb, pba, conv_w, a_log, dt_bias, norm_g, batch):
    t = pb.shape[0]
    s = t // batch
    tt = min(MIX_TILE, s)
    nt = s // tt
    it = np.arange(tt)
    tri = jnp.asarray((it[:, None] // CHUNK == it[None, :] // CHUNK) & (it[:, None] >= it[None, :]), BF16)
    alog = jnp.zeros((1, LANES), F32).at[0, B_HEADS:2 * B_HEADS].set(a_log.astype(F32))
    dtb = jnp.zeros((1, LANES), F32).at[0, B_HEADS:2 * B_HEADS].set(dt_bias.astype(F32))
    g = norm_g.astype(F32).reshape(1, B_HEAD_DIM)
    row = lambda b, i: (b * nt + i, 0)
    const = lambda b, i: (0, 0)
    return pl.pallas_call(
        _gdn_kernel,
        grid=(batch, nt),
        in_specs=[pl.BlockSpec((tt, PB_W), row), pl.BlockSpec((tt, PBA_W), row),
                  pl.BlockSpec(conv_w.shape, const), pl.BlockSpec((1, LANES), const),
                  pl.BlockSpec((1, LANES), const), pl.BlockSpec((1, B_HEAD_DIM), const),
                  pl.BlockSpec((tt, tt), const)],
        out_specs=pl.BlockSpec((tt, B_WIDTH), row),
        out_shape=jax.ShapeDtypeStruct((t, B_WIDTH), BF16),
        scratch_shapes=[pltpu.VMEM((tt + SUBLANES, 3 * B_WIDTH), F32),
                        pltpu.VMEM((B_HEADS, B_HEAD_DIM, B_HEAD_DIM), F32)],
        compiler_params=pltpu.CompilerParams(dimension_semantics=("arbitrary", "arbitrary"),
                                             vmem_limit_bytes=VMEM_LIMIT),
        name="gdn",
    )(pb, pba, conv_w.astype(F32), alog, dtb, g, tri)


def _rglru_kernel(pc_ref, cw_ref, cb_ref, wa_ref, ba_ref, wx_ref, bx_ref, lam_ref, o_ref,
                  xbuf_ref, h_ref):
    tt = pc_ref.shape[0]
    w = C_WIDTH
    pad = SUBLANES
    first = pl.program_id(1) == 0

    @pl.when(first)
    def _():
        h_ref[...] = jnp.zeros_like(h_ref)
        xbuf_ref[0:pad, :] = jnp.zeros((pad, w), F32)

    xbuf_ref[pad:pad + tt, :] = pc_ref[:, 0:w]
    xc = xbuf_ref[pad - CONV_K + 1:pad - CONV_K + 1 + tt, :] * cw_ref[0:1, :]
    for j in range(1, CONV_K):
        xc = xc + xbuf_ref[pad - CONV_K + 1 + j:pad - CONV_K + 1 + j + tt, :] * cw_ref[j:j + 1, :]
    xbuf_ref[0:pad, :] = xbuf_ref[tt:tt + pad, :]
    xc = xc + cb_ref[...]

    x_bf = xc.astype(BF16)
    r = _sigmoid(_dot(x_bf, wa_ref[...]) + ba_ref[...])
    gate_i = _sigmoid(_dot(x_bf, wx_ref[...]) + bx_ref[...])
    log_a = -RG_C * r * _softplus(-lam_ref[...])
    a = jnp.exp(log_a)
    row = lax.broadcasted_iota(jnp.int32, (tt, w), 0)
    mult = jnp.sqrt(jnp.maximum(-jnp.tanh(log_a) * (a * a + 1.0), EPS))
    mult = jnp.where(row + pl.program_id(1) * tt == 0, 1.0, mult)
    b = mult * gate_i * xc

    d = 1
    while d < tt:
        keep = row >= d
        a_prev = jnp.where(keep, pltpu.roll(a, d, axis=0), 1.0)
        b_prev = jnp.where(keep, pltpu.roll(b, d, axis=0), 0.0)
        b = a * b_prev + b
        a = a * a_prev
        d *= 2
    h = b + a * h_ref[0:1, :]
    h_ref[...] = jnp.broadcast_to(h[tt - 1:tt, :], h_ref.shape)

    y = pc_ref[:, w:2 * w]
    gelu = 0.5 * y * (1.0 + jnp.tanh(0.7978845608028654 * (y + 0.044715 * (y * y * y))))
    o_ref[...] = (gelu * h).astype(o_ref.dtype)


def _block_diag(wb):
    n, d, _ = wb.shape
    out = jnp.zeros((n * d, n * d), wb.dtype)
    for i in range(n):
        out = out.at[i * d:(i + 1) * d, i * d:(i + 1) * d].set(wb[i])
    return out


def _rglru(pc, conv_w, conv_b, w_a, b_a, w_x, b_x, lam, batch):
    t = pc.shape[0]
    s = t // batch
    tt = min(MIX_TILE, s)
    nt = s // tt
    w = C_WIDTH
    vec = lambda a: a.astype(F32).reshape(1, w)
    row = lambda b, i: (b * nt + i, 0)
    const = lambda b, i: (0, 0)
    vspec = pl.BlockSpec((1, w), const)
    mspec = pl.BlockSpec((w, w), const)
    return pl.pallas_call(
        _rglru_kernel,
        grid=(batch, nt),
        in_specs=[pl.BlockSpec((tt, PC_W), row), pl.BlockSpec(conv_w.shape, const), vspec,
                  mspec, vspec, mspec, vspec, vspec],
        out_specs=pl.BlockSpec((tt, w), row),
        out_shape=jax.ShapeDtypeStruct((t, w), BF16),
        scratch_shapes=[pltpu.VMEM((tt + SUBLANES, w), F32), pltpu.VMEM((SUBLANES, w), F32)],
        compiler_params=pltpu.CompilerParams(dimension_semantics=("arbitrary", "arbitrary"),
                                             vmem_limit_bytes=VMEM_LIMIT),
        name="rglru",
    )(pc, conv_w.astype(F32), vec(conv_b), _block_diag(w_a).astype(BF16), vec(b_a),
      _block_diag(w_x).astype(BF16), vec(b_x), vec(lam))


def _reorder_w_in(w):
    a_end = PA_W
    b_end = a_end + PB_W
    ba_end = b_end + 2 * B_HEADS
    pad = jnp.zeros((w.shape[0], PBA_W - 2 * B_HEADS), w.dtype)
    return jnp.concatenate([w[:, :b_end], w[:, ba_end:], w[:, b_end:ba_end], pad], axis=1).astype(BF16)


def kernel(x, norm1_g, w_in, hgrn_lb_logits, hgrn_norm_g, gdn_conv_w, gdn_a_log, gdn_dt_bias, gdn_norm_g, lru_conv_w, lru_conv_b, lru_w_a, lru_b_a, lru_w_x, lru_b_x, lru_lambda, w_out, norm2_g, w_up, w_down, final_norm_g):
    bsz, seq, d = x.shape
    depth = w_in.shape[0]
    xf = x.reshape(bsz * seq, d).astype(F32)
    gf = final_norm_g.astype(F32).reshape(1, d)
    for l in range(depth):
        pa, pb, pc, pba = _inproj(xf, norm1_g[l].astype(F32).reshape(1, d), _reorder_w_in(w_in[l]))
        oa = _hgrn2(pa, hgrn_lb_logits, hgrn_norm_g[l], l, bsz)
        ob = _gdn(pb, pba, gdn_conv_w[l], gdn_a_log[l], gdn_dt_bias[l], gdn_norm_g[l], bsz)
        oc = _rglru(pc, lru_conv_w[l], lru_conv_b[l], lru_w_a[l], lru_b_a[l], lru_w_x[l], lru_b_x[l],
                    lru_lambda[l], bsz)
        xf = _outmlp(xf, oa, ob, oc, w_out[l].astype(BF16), norm2_g[l].astype(F32).reshape(1, d),
                     w_up[l].astype(BF16), w_down[l].astype(BF16), gf, l == depth - 1)
    return xf.reshape(bsz, seq, d).astype(x.dtype)
```

```python
import functools

import jax
import jax.numpy as jnp
import numpy as np
from jax import lax
from jax.experimental import pallas as pl
from jax.experimental.pallas import tpu as pltpu

F32 = jnp.float32
BF16 = jnp.bfloat16

A_HEADS = 4
A_HEAD_DIM = 64
A_WIDTH = A_HEADS * A_HEAD_DIM
B_HEADS = 4
B_HEAD_DIM = 128
B_WIDTH = B_HEADS * B_HEAD_DIM
C_BLOCKS = 4
C_WIDTH = 256
C_BLOCK_DIM = C_WIDTH // C_BLOCKS
CONV_K = 4
CHUNK = 64
SUB = 16
RG_C = 8.0
EPS = 1e-6
TINY = 1e-30

LANES = 128
SUBLANES = 8
VMEM_LIMIT = 56 * 1024 * 1024

PROJ_TILE = 512
MIX_TILE = 256
FF_CHUNK = 1024

PA_W = 4 * A_WIDTH
PB_W = 4 * B_WIDTH
PC_W = 2 * C_WIDTH
PBA_W = LANES


def _sigmoid(x):
    return 1.0 / (1.0 + jnp.exp(-x))


def _silu(x):
    return x * _sigmoid(x)


def _softplus(x):
    return jnp.maximum(x, 0.0) + jnp.log1p(jnp.exp(-jnp.abs(x)))


def _split3(x):
    hi = x.astype(BF16)
    r1 = x - hi.astype(F32)
    mid = r1.astype(BF16)
    lo = (r1 - mid.astype(F32)).astype(BF16)
    return hi, mid, lo


def _dot(a, b):
    return jnp.dot(a, b, preferred_element_type=F32)


def _dot_nt(a, b):
    return lax.dot_general(a, b, (((1,), (1,)), ((), ())), preferred_element_type=F32)


def _dot_tn(a, b):
    return lax.dot_general(a, b, (((0,), (0,)), ((), ())), preferred_element_type=F32)


def _bdot(a, b):
    return lax.dot_general(a, b, (((2,), (1,)), ((0,), (0,))), preferred_element_type=F32)


def _bdot_nt(a, b):
    return lax.dot_general(a, b, (((2,), (2,)), ((0,), (0,))), preferred_element_type=F32)


def _mask_dot(m01, x):
    hi, mid, lo = _split3(x)
    return _dot(m01, hi) + _dot(m01, mid) + _dot(m01, lo)


def _dot_mask(x, m01):
    hi, mid, lo = _split3(x)
    return _dot(hi, m01) + _dot(mid, m01) + _dot(lo, m01)


def _rms_scale(x):
    return x * lax.rsqrt(jnp.mean(x * x, axis=-1, keepdims=True) + EPS)


def _inproj_kernel(x_ref, g_ref, w_ref, pa_ref, pb_ref, pc_ref, pba_ref):
    h = (_rms_scale(x_ref[...]) * g_ref[...]).astype(BF16)
    off = 0
    for ref, width in ((pa_ref, PA_W), (pb_ref, PB_W), (pc_ref, PC_W), (pba_ref, PBA_W)):
        ref[...] = _dot(h, w_ref[:, off:off + width])
        off += width


def _inproj(xf, g, w):
    t, d = xf.shape
    n = w.shape[1]
    tm = min(PROJ_TILE, t)
    row = lambda i: (i, 0)
    const = lambda i: (0, 0)
    return pl.pallas_call(
        _inproj_kernel,
        grid=(t // tm,),
        in_specs=[pl.BlockSpec((tm, d), row), pl.BlockSpec((1, d), const),
                  pl.BlockSpec((d, n), const)],
        out_specs=[pl.BlockSpec((tm, PA_W), row), pl.BlockSpec((tm, PB_W), row),
                   pl.BlockSpec((tm, PC_W), row), pl.BlockSpec((tm, PBA_W), row)],
        out_shape=[jax.ShapeDtypeStruct((t, PA_W), F32), jax.ShapeDtypeStruct((t, PB_W), F32),
                   jax.ShapeDtypeStruct((t, PC_W), F32), jax.ShapeDtypeStruct((t, PBA_W), F32)],
        compiler_params=pltpu.CompilerParams(dimension_semantics=("arbitrary",),
                                             vmem_limit_bytes=VMEM_LIMIT),
        name="inproj",
    )(xf, g, w)


def _outmlp_kernel(x_ref, oa_ref, ob_ref, oc_ref, wo_ref, g2_ref, wup_ref, wdn_ref, gf_ref,
                   o_ref, *, final):
    x1 = x_ref[...]
    x1 = x1 + _dot(oa_ref[...], wo_ref[0:A_WIDTH, :])
    x1 = x1 + _dot(ob_ref[...], wo_ref[A_WIDTH:A_WIDTH + B_WIDTH, :])
    x1 = x1 + _dot(oc_ref[...], wo_ref[A_WIDTH + B_WIDTH:, :])
    o_ref[...] = x1
    h2 = (_rms_scale(x1) * g2_ref[...]).astype(BF16)
    d_ff = wup_ref.shape[1]
    for c in range(d_ff // FF_CHUNK):
        u = jnp.maximum(_dot(h2, wup_ref[:, c * FF_CHUNK:(c + 1) * FF_CHUNK]), 0.0)
        o_ref[...] += _dot((u * u).astype(BF16), wdn_ref[c * FF_CHUNK:(c + 1) * FF_CHUNK, :])
    if final:
        o_ref[...] = _rms_scale(o_ref[...]) * gf_ref[...]


def _outmlp(xf, oa, ob, oc, wo, g2, wup, wdn, gf, final):
    t, d = xf.shape
    tm = min(PROJ_TILE, t)
    row = lambda i: (i, 0)
    const = lambda i: (0, 0)
    full = lambda a: pl.BlockSpec(a.shape, const)
    return pl.pallas_call(
        functools.partial(_outmlp_kernel, final=final),
        grid=(t // tm,),
        in_specs=[pl.BlockSpec((tm, d), row), pl.BlockSpec((tm, A_WIDTH), row),
                  pl.BlockSpec((tm, B_WIDTH), row), pl.BlockSpec((tm, C_WIDTH), row),
                  full(wo), full(g2), full(wup), full(wdn), full(gf)],
        out_specs=pl.BlockSpec((tm, d), row),
        out_shape=jax.ShapeDtypeStruct((t, d), F32),
        compiler_params=pltpu.CompilerParams(dimension_semantics=("arbitrary",),
                                             vmem_limit_bytes=VMEM_LIMIT),
        name="outmlp",
    )(xf, oa, ob, oc, wo, g2, wup, wdn, gf)


def _hgrn2_kernel(pa_ref, lbl_ref, g_ref, ones_ref, tri_ref, o_ref, st_ref, *, layer):
    tt = pa_ref.shape[0]
    w = A_WIDTH
    nsub = tt // SUB

    @pl.when(pl.program_id(1) == 0)
    def _():
        st_ref[...] = jnp.zeros_like(st_ref)

    logits = lbl_ref[...]
    ex = jnp.exp(logits - jnp.max(logits, axis=0, keepdims=True))
    p = ex / jnp.sum(ex, axis=0, keepdims=True)
    lb = jnp.clip(jnp.sum(p[0:layer + 1], axis=0, keepdims=True) - p[0:1], 0.0, 1.0 - EPS)

    q = _silu(pa_ref[:, 0:w])
    fp = pa_ref[:, w:2 * w]
    v = pa_ref[:, 2 * w:3 * w]
    gate = pa_ref[:, 3 * w:4 * w]
    f = lb + (1.0 - lb) * _sigmoid(fp)
    logf = jnp.log(jnp.maximum(f, TINY))
    k = (1.0 - lb) * _sigmoid(-fp)

    ones_bd = ones_ref[...]
    cum = _mask_dot(tri_ref[...], logf)

    q3 = q.reshape(nsub, SUB, w)
    k3 = k.reshape(nsub, SUB, w)
    v3 = v.reshape(nsub, SUB, w)
    c3 = cum.reshape(nsub, SUB, w)
    row = lax.broadcasted_iota(jnp.int32, (nsub, SUB, w), 1)

    o_intra = jnp.zeros((nsub, SUB, w), F32)
    for j in range(SUB):
        keep = row >= j
        dec = jnp.where(keep, jnp.exp(jnp.where(keep, c3 - c3[:, j:j + 1, :], 0.0)), 0.0)
        x = (q3 * k3[:, j:j + 1, :] * dec).reshape(tt, w).astype(BF16)
        attn = _dot(x, ones_bd)
        o_intra = o_intra + attn.reshape(nsub, SUB, w) * v3[:, j:j + 1, :]

    c_last = c3[:, SUB - 1:SUB, :]
    qd = (q3 * jnp.exp(c3)).astype(BF16)
    kd = (k3 * jnp.exp(c_last - c3)).astype(BF16)
    blk = (lax.broadcasted_iota(jnp.int32, (w, w), 0) // A_HEAD_DIM
           == lax.broadcasted_iota(jnp.int32, (w, w), 1) // A_HEAD_DIM)
    st = st_ref[...]
    o_parts = []
    for i in range(nsub):
        o_parts.append(_dot_nt(qd[i], st.astype(BF16)) + o_intra[i])
        upd = _dot_tn(v3[i].astype(BF16), kd[i])
        st = st * jnp.exp(c_last[i]) + jnp.where(blk, upd, 0.0)
    st_ref[...] = st
    o = jnp.concatenate(o_parts, axis=0)

    ms = _dot_mask(o * o, ones_bd) * (1.0 / A_HEAD_DIM)
    o_ref[...] = (o * lax.rsqrt(ms + EPS) * g_ref[...] * _silu(gate)).astype(o_ref.dtype)


def _hgrn2(pa, lb_logits, norm_g, layer, batch):
    t = pa.shape[0]
    s = t // batch
    tt = min(MIX_TILE, s)
    nt = s // tt
    w = A_WIDTH
    idx = np.arange(w)
    ones_bd = jnp.asarray(idx[:, None] // A_HEAD_DIM == idx[None, :] // A_HEAD_DIM, BF16)
    it = np.arange(tt)
    tri = jnp.asarray((it[:, None] // SUB == it[None, :] // SUB) & (it[:, None] >= it[None, :]), BF16)
    g = jnp.tile(norm_g.astype(F32), A_HEADS).reshape(1, w)
    row = lambda b, i: (b * nt + i, 0)
    const = lambda b, i: (0, 0)
    return pl.pallas_call(
        functools.partial(_hgrn2_kernel, layer=layer),
        grid=(batch, nt),
        in_specs=[pl.BlockSpec((tt, PA_W), row), pl.BlockSpec(lb_logits.shape, const),
                  pl.BlockSpec((1, w), const), pl.BlockSpec((w, w), const),
                  pl.BlockSpec((tt, tt), const)],
        out_specs=pl.BlockSpec((tt, w), row),
        out_shape=jax.ShapeDtypeStruct((t, w), BF16),
        scratch_shapes=[pltpu.VMEM((w, w), F32)],
        compiler_params=pltpu.CompilerParams(dimension_semantics=("arbitrary", "arbitrary"),
                                             vmem_limit_bytes=VMEM_LIMIT),
        name="hgrn2",
    )(pa, lb_logits.astype(F32), g, ones_bd, tri)


def _gdn_kernel(pb_ref, pba_ref, cw_ref, alog_ref, dtb_ref, g_ref, tri_ref, o_ref,
                xbuf_ref, s_ref):
    tt = pb_ref.shape[0]
    hd = B_HEAD_DIM
    nchunk = tt // CHUNK
    qkv_w = 3 * B_WIDTH
    pad = SUBLANES

    @pl.when(pl.program_id(1) == 0)
    def _():
        s_ref[...] = jnp.zeros_like(s_ref)
        xbuf_ref[0:pad, :] = jnp.zeros((pad, qkv_w), F32)

    xbuf_ref[pad:pad + tt, :] = pb_ref[:, 0:qkv_w]
    y = xbuf_ref[pad - CONV_K + 1:pad - CONV_K + 1 + tt, :] * cw_ref[0:1, :]
    for j in range(1, CONV_K):
        y = y + xbuf_ref[pad - CONV_K + 1 + j:pad - CONV_K + 1 + j + tt, :] * cw_ref[j:j + 1, :]
    xbuf_ref[0:pad, :] = xbuf_ref[tt:tt + pad, :]
    qkv = _silu(y)

    lane = lax.broadcasted_iota(jnp.int32, (1, LANES), 1)
    is_alpha = (lane >= B_HEADS) & (lane < 2 * B_HEADS)
    ba = pba_ref[...]
    beta_all = _sigmoid(ba)
    rate = jnp.where(is_alpha, jnp.exp(alog_ref[...]), 0.0)
    log_alpha = -rate * _softplus(ba + dtb_ref[...])
    gc_all = _mask_dot(tri_ref[...], log_alpha)
    gc_t = gc_all.T

    r_i = lax.broadcasted_iota(jnp.int32, (CHUNK, CHUNK), 0)
    c_i = lax.broadcasted_iota(jnp.int32, (CHUNK, CHUNK), 1)
    causal = r_i >= c_i
    strict = r_i > c_i
    eye = (r_i == c_i).astype(F32)
    scale = hd ** -0.5

    def chunks(xs):
        return jnp.concatenate([x.reshape(nchunk, CHUNK, x.shape[-1]) for x in xs], axis=0)

    q_l, k_l, v_l, beta_l, g_l, grow_l = [], [], [], [], [], []
    for h in range(B_HEADS):
        qh = qkv[:, h * hd:(h + 1) * hd]
        kh = qkv[:, B_WIDTH + h * hd:B_WIDTH + (h + 1) * hd]
        q_l.append(qh * lax.rsqrt(jnp.sum(qh * qh, axis=-1, keepdims=True) + EPS))
        k_l.append(kh * lax.rsqrt(jnp.sum(kh * kh, axis=-1, keepdims=True) + EPS))
        v_l.append(qkv[:, 2 * B_WIDTH + h * hd:2 * B_WIDTH + (h + 1) * hd])
        beta_l.append(jnp.broadcast_to(beta_all[:, h:h + 1], (tt, hd)))
        g_l.append(jnp.broadcast_to(gc_all[:, B_HEADS + h:B_HEADS + h + 1], (tt, hd)))
        for c in range(nchunk):
            grow_l.append(gc_t[B_HEADS + h:B_HEADS + h + 1, c * CHUNK:(c + 1) * CHUNK][None])
    q3, k3, v3, beta3, g3 = chunks(q_l), chunks(k_l), chunks(v_l), chunks(beta_l), chunks(g_l)
    g_row3 = jnp.concatenate(grow_l, axis=0)
    eg3 = jnp.exp(g3)
    g_last3 = g3[:, CHUNK - 1:CHUNK, :]
    kb3 = k3 * beta3
    k_bf = k3.astype(BF16)
    decay = jnp.where(causal, jnp.exp(jnp.where(causal, g3[:, :, 0:CHUNK] - g_row3, 0.0)), 0.0)
    kq = _bdot_nt(jnp.concatenate([kb3, q3], axis=1).astype(BF16), k_bf)
    a_mat = jnp.where(strict, kq[:, 0:CHUNK] * decay, 0.0)
    qk = kq[:, CHUNK:2 * CHUNK] * scale * decay
    t_inv = eye - jnp.where(r_i // 2 == c_i // 2, a_mat, 0.0)
    b = 2
    while b < CHUNK:
        off = jnp.where((r_i // (2 * b) == c_i // (2 * b)) & (r_i // b != c_i // b), a_mat, 0.0)
        t_bf = t_inv.astype(BF16)
        t_inv = t_inv - _bdot(t_bf, _bdot(off.astype(BF16), t_bf).astype(BF16))
        b *= 2
    rhs = jnp.concatenate([v3 * beta3, kb3 * eg3], axis=2).astype(BF16)
    uw = _bdot(t_inv.astype(BF16), rhs)
    qd3 = q3 * (scale * eg3)
    kd3 = k3 * jnp.exp(g_last3 - g3)
    sdec3 = jnp.exp(g_last3)

    states = [s_ref[h] for h in range(B_HEADS)]
    o_l = [[None] * nchunk for _ in range(B_HEADS)]
    for c in range(nchunk):
        for h in range(B_HEADS):
            i = h * nchunk + c
            m1 = _dot(jnp.concatenate([uw[i, :, hd:2 * hd], qd3[i]], axis=0).astype(BF16),
                      states[h].astype(BF16))
            v_new = uw[i, :, 0:hd] - m1[0:CHUNK]
            m2 = _dot(jnp.concatenate([qk[i], kd3[i].T], axis=0).astype(BF16), v_new.astype(BF16))
            o_l[h][c] = m1[CHUNK:2 * CHUNK] + m2[0:CHUNK]
            states[h] = states[h] * sdec3[i] + m2[CHUNK:CHUNK + hd]
    for h in range(B_HEADS):
        s_ref[h] = states[h]
        o = jnp.concatenate(o_l[h], axis=0)
        z = pb_ref[:, qkv_w + h * hd:qkv_w + (h + 1) * hd]
        o_ref[:, h * hd:(h + 1) * hd] = (_rms_scale(o) * g_ref[...] * _silu(z)).astype(o_ref.dtype)


def _gdn(pb, pba, conv_w, a_log, dt_bias, norm_g, batch):
    t = pb.shape[0]
    s = t // batch
    tt = min(MIX_TILE, s)
    nt = s // tt
    it = np.arange(tt)
    tri = jnp.asarray((it[:, None] // CHUNK == it[None, :] // CHUNK) & (it[:, None] >= it[None, :]), BF16)
    alog = jnp.zeros((1, LANES), F32).at[0, B_HEADS:2 * B_HEADS].set(a_log.astype(F32))
    dtb = jnp.zeros((1, LANES), F32).at[0, B_HEADS:2 * B_HEADS].set(dt_bias.astype(F32))
    g = norm_g.astype(F32).reshape(1, B_HEAD_DIM)
    row = lambda b, i: (b * nt + i, 0)
    const = lambda b, i: (0, 0)
    return pl.pallas_call(
        _gdn_kernel,
        grid=(batch, nt),
        in_specs=[pl.BlockSpec((tt, PB_W), row), pl.BlockSpec((tt, PBA_W), row),
                  pl.BlockSpec(conv_w.shape, const), pl.BlockSpec((1, LANES), const),
                  pl.BlockSpec((1, LANES), const), pl.BlockSpec((1, B_HEAD_DIM), const),
                  pl.BlockSpec((tt, tt), const)],
        out_specs=pl.BlockSpec((tt, B_WIDTH), row),
        out_shape=jax.ShapeDtypeStruct((t, B_WIDTH), BF16),
        scratch_shapes=[pltpu.VMEM((tt + SUBLANES, 3 * B_WIDTH), F32),
                        pltpu.VMEM((B_HEADS, B_HEAD_DIM, B_HEAD_DIM), F32)],
        compiler_params=pltpu.CompilerParams(dimension_semantics=("arbitrary", "arbitrary"),
                                             vmem_limit_bytes=VMEM_LIMIT),
        name="gdn",
    )(pb, pba, conv_w.astype(F32), alog, dtb, g, tri)


def _rglru_kernel(pc_ref, cw_ref, cb_ref, wa_ref, ba_ref, wx_ref, bx_ref, lam_ref, o_ref,
                  xbuf_ref, h_ref):
    tt = pc_ref.shape[0]
    w = C_WIDTH
    pad = SUBLANES
    first = pl.program_id(1) == 0

    @pl.when(first)
    def _():
        h_ref[...] = jnp.zeros_like(h_ref)
        xbuf_ref[0:pad, :] = jnp.zeros((pad, w), F32)

    xbuf_ref[pad:pad + tt, :] = pc_ref[:, 0:w]
    xc = xbuf_ref[pad - CONV_K + 1:pad - CONV_K + 1 + tt, :] * cw_ref[0:1, :]
    for j in range(1, CONV_K):
        xc = xc + xbuf_ref[pad - CONV_K + 1 + j:pad - CONV_K + 1 + j + tt, :] * cw_ref[j:j + 1, :]
    xbuf_ref[0:pad, :] = xbuf_ref[tt:tt + pad, :]
    xc = xc + cb_ref[...]

    x_bf = xc.astype(BF16)
    r = _sigmoid(_dot(x_bf, wa_ref[...]) + ba_ref[...])
    gate_i = _sigmoid(_dot(x_bf, wx_ref[...]) + bx_ref[...])
    log_a = -RG_C * r * _softplus(-lam_ref[...])
    a = jnp.exp(log_a)
    row = lax.broadcasted_iota(jnp.int32, (tt, w), 0)
    mult = jnp.sqrt(jnp.maximum(-jnp.tanh(log_a) * (a * a + 1.0), EPS))
    mult = jnp.where(row + pl.program_id(1) * tt == 0, 1.0, mult)
    b = mult * gate_i * xc

    d = 1
    while d < tt:
        keep = row >= d
        a_prev = jnp.where(keep, pltpu.roll(a, d, axis=0), 1.0)
        b_prev = jnp.where(keep, pltpu.roll(b, d, axis=0), 0.0)
        b = a * b_prev + b
        a = a * a_prev
        d *= 2
    h = b + a * h_ref[0:1, :]
    h_ref[...] = jnp.broadcast_to(h[tt - 1:tt, :], h_ref.shape)

    y = pc_ref[:, w:2 * w]
    gelu = 0.5 * y * (1.0 + jnp.tanh(0.7978845608028654 * (y + 0.044715 * (y * y * y))))
    o_ref[...] = (gelu * h).astype(o_ref.dtype)


def _block_diag(wb):
    n, d, _ = wb.shape
    out = jnp.zeros((n * d, n * d), wb.dtype)
    for i in range(n):
        out = out.at[i * d:(i + 1) * d, i * d:(i + 1) * d].set(wb[i])
    return out


def _rglru(pc, conv_w, conv_b, w_a, b_a, w_x, b_x, lam, batch):
    t = pc.shape[0]
    s = t // batch
    tt = min(MIX_TILE, s)
    nt = s // tt
    w = C_WIDTH
    vec = lambda a: a.astype(F32).reshape(1, w)
    row = lambda b, i: (b * nt + i, 0)
    const = lambda b, i: (0, 0)
    vspec = pl.BlockSpec((1, w), const)
    mspec = pl.BlockSpec((w, w), const)
    return pl.pallas_call(
        _rglru_kernel,
        grid=(batch, nt),
        in_specs=[pl.BlockSpec((tt, PC_W), row), pl.BlockSpec(conv_w.shape, const), vspec,
                  mspec, vspec, mspec, vspec, vspec],
        out_specs=pl.BlockSpec((tt, w), row),
        out_shape=jax.ShapeDtypeStruct((t, w), BF16),
        scratch_shapes=[pltpu.VMEM((tt + SUBLANES, w), F32), pltpu.VMEM((SUBLANES, w), F32)],
        compiler_params=pltpu.CompilerParams(dimension_semantics=("arbitrary", "arbitrary"),
                                             vmem_limit_bytes=VMEM_LIMIT),
        name="rglru",
    )(pc, conv_w.astype(F32), vec(conv_b), _block_diag(w_a).astype(BF16), vec(b_a),
      _block_diag(w_x).astype(BF16), vec(b_x), vec(lam))


def _reorder_w_in(w):
    a_end = PA_W
    b_end = a_end + PB_W
    ba_end = b_end + 2 * B_HEADS
    pad = jnp.zeros((w.shape[0], PBA_W - 2 * B_HEADS), w.dtype)
    return jnp.concatenate([w[:, :b_end], w[:, ba_end:], w[:, b_end:ba_end], pad], axis=1).astype(BF16)


def kernel(x, norm1_g, w_in, hgrn_lb_logits, hgrn_norm_g, gdn_conv_w, gdn_a_log, gdn_dt_bias, gdn_norm_g, lru_conv_w, lru_conv_b, lru_w_a, lru_b_a, lru_w_x, lru_b_x, lru_lambda, w_out, norm2_g, w_up, w_down, final_norm_g):
    bsz, seq, d = x.shape
    depth = w_in.shape[0]
    xf = x.reshape(bsz * seq, d).astype(F32)
    gf = final_norm_g.astype(F32).reshape(1, d)
    for l in range(depth):
        pa, pb, pc, pba = _inproj(xf, norm1_g[l].astype(F32).reshape(1, d), _reorder_w_in(w_in[l]))
        oa = _hgrn2(pa, hgrn_lb_logits, hgrn_norm_g[l], l, bsz)
        ob = _gdn(pb, pba, gdn_conv_w[l], gdn_a_log[l], gdn_dt_bias[l], gdn_norm_g[l], bsz)
        oc = _rglru(pc, lru_conv_w[l], lru_conv_b[l], lru_w_a[l], lru_b_a[l], lru_w_x[l], lru_b_x[l],
                    lru_lambda[l], bsz)
        xf = _outmlp(xf, oa, ob, oc, w_out[l].astype(BF16), norm2_g[l].astype(F32).reshape(1, d),
                     w_up[l].astype(BF16), w_down[l].astype(BF16), gf, l == depth - 1)
    return xf.reshape(bsz, seq, d).astype(x.dtype)
```

```python
import functools

import jax
import jax.numpy as jnp
import numpy as np
from jax import lax
from jax.experimental import pallas as pl
from jax.experimental.pallas import tpu as pltpu

F32 = jnp.float32
BF16 = jnp.bfloat16

A_HEADS = 4
A_HEAD_DIM = 64
A_WIDTH = A_HEADS * A_HEAD_DIM
B_HEADS = 4
B_HEAD_DIM = 128
B_WIDTH = B_HEADS * B_HEAD_DIM
C_BLOCKS = 4
C_WIDTH = 256
C_BLOCK_DIM = C_WIDTH // C_BLOCKS
CONV_K = 4
CHUNK = 64
SUB = 16
RG_C = 8.0
EPS = 1e-6
TINY = 1e-30

LANES = 128
SUBLANES = 8
VMEM_LIMIT = 56 * 1024 * 1024

PROJ_TILE = 512
MIX_TILE = 256
MIX_BATCH = 4
HGRN_TILE = 128
GDN_TILE = 128
FF_CHUNK = 1024

PA_W = 4 * A_WIDTH
PB_W = 4 * B_WIDTH
PC_W = 2 * C_WIDTH
PBA_W = LANES


def _sigmoid(x):
    return 1.0 / (1.0 + jnp.exp(-x))


def _silu(x):
    return x * _sigmoid(x)


def _softplus(x):
    return jnp.maximum(x, 0.0) + jnp.log1p(jnp.exp(-jnp.abs(x)))


def _split3(x):
    hi = x.astype(BF16)
    r1 = x - hi.astype(F32)
    mid = r1.astype(BF16)
    lo = (r1 - mid.astype(F32)).astype(BF16)
    return hi, mid, lo


def _dot(a, b):
    return jnp.dot(a, b, preferred_element_type=F32)


def _dot_nt(a, b):
    return lax.dot_general(a, b, (((1,), (1,)), ((), ())), preferred_element_type=F32)


def _dot_tn(a, b):
    return lax.dot_general(a, b, (((0,), (0,)), ((), ())), preferred_element_type=F32)


def _bdot(a, b):
    return lax.dot_general(a, b, (((2,), (1,)), ((0,), (0,))), preferred_element_type=F32)


def _bdot_nt(a, b):
    return lax.dot_general(a, b, (((2,), (2,)), ((0,), (0,))), preferred_element_type=F32)


def _mask_dot(m01, x):
    hi, mid, lo = _split3(x)
    return _dot(m01, hi) + _dot(m01, mid) + _dot(m01, lo)


def _dot_mask(x, m01):
    hi, mid, lo = _split3(x)
    return _dot(hi, m01) + _dot(mid, m01) + _dot(lo, m01)


def _rms_scale(x):
    return x * lax.rsqrt(jnp.mean(x * x, axis=-1, keepdims=True) + EPS)


def _inproj_kernel(x_ref, g_ref, w_ref, pa_ref, pb_ref, pc_ref, pba_ref):
    h = (_rms_scale(x_ref[...]) * g_ref[...]).astype(BF16)
    off = 0
    for ref, width in ((pa_ref, PA_W), (pb_ref, PB_W), (pc_ref, PC_W), (pba_ref, PBA_W)):
        ref[...] = _dot(h, w_ref[:, off:off + width])
        off += width


def _inproj(xf, g, w):
    t, d = xf.shape
    n = w.shape[1]
    tm = min(PROJ_TILE, t)
    row = lambda i: (i, 0)
    const = lambda i: (0, 0)
    return pl.pallas_call(
        _inproj_kernel,
        grid=(t // tm,),
        in_specs=[pl.BlockSpec((tm, d), row), pl.BlockSpec((1, d), const),
                  pl.BlockSpec((d, n), const)],
        out_specs=[pl.BlockSpec((tm, PA_W), row), pl.BlockSpec((tm, PB_W), row),
                   pl.BlockSpec((tm, PC_W), row), pl.BlockSpec((tm, PBA_W), row)],
        out_shape=[jax.ShapeDtypeStruct((t, PA_W), F32), jax.ShapeDtypeStruct((t, PB_W), F32),
                   jax.ShapeDtypeStruct((t, PC_W), F32), jax.ShapeDtypeStruct((t, PBA_W), F32)],
        compiler_params=pltpu.CompilerParams(dimension_semantics=("arbitrary",),
                                             vmem_limit_bytes=VMEM_LIMIT),
        name="inproj",
    )(xf, g, w)


def _outmlp_kernel(x_ref, oa_ref, ob_ref, oc_ref, wo_ref, g2_ref, wup_ref, wdn_ref, gf_ref,
                   o_ref, *, final):
    x1 = x_ref[...]
    x1 = x1 + _dot(oa_ref[...], wo_ref[0:A_WIDTH, :])
    x1 = x1 + _dot(ob_ref[...], wo_ref[A_WIDTH:A_WIDTH + B_WIDTH, :])
    x1 = x1 + _dot(oc_ref[...], wo_ref[A_WIDTH + B_WIDTH:, :])
    o_ref[...] = x1
    h2 = (_rms_scale(x1) * g2_ref[...]).astype(BF16)
    d_ff = wup_ref.shape[1]
    for c in range(d_ff // FF_CHUNK):
        u = jnp.maximum(_dot(h2, wup_ref[:, c * FF_CHUNK:(c + 1) * FF_CHUNK]), 0.0)
        o_ref[...] += _dot((u * u).astype(BF16), wdn_ref[c * FF_CHUNK:(c + 1) * FF_CHUNK, :])
    if final:
        o_ref[...] = _rms_scale(o_ref[...]) * gf_ref[...]


def _outmlp(xf, oa, ob, oc, wo, g2, wup, wdn, gf, final):
    t, d = xf.shape
    tm = min(PROJ_TILE, t)
    row = lambda i: (i, 0)
    const = lambda i: (0, 0)
    full = lambda a: pl.BlockSpec(a.shape, const)
    return pl.pallas_call(
        functools.partial(_outmlp_kernel, final=final),
        grid=(t // tm,),
        in_specs=[pl.BlockSpec((tm, d), row), pl.BlockSpec((tm, A_WIDTH), row),
                  pl.BlockSpec((tm, B_WIDTH), row), pl.BlockSpec((tm, C_WIDTH), row),
                  full(wo), full(g2), full(wup), full(wdn), full(gf)],
        out_specs=pl.BlockSpec((tm, d), row),
        out_shape=jax.ShapeDtypeStruct((t, d), F32),
        compiler_params=pltpu.CompilerParams(dimension_semantics=("arbitrary",),
                                             vmem_limit_bytes=VMEM_LIMIT),
        name="outmlp",
    )(xf, oa, ob, oc, wo, g2, wup, wdn, gf)


def _hgrn2_kernel(pa_ref, lbl_ref, g_ref, ones_ref, tri_ref, o_ref, st_ref, *, layer):
    nbat, tt = pa_ref.shape[0], pa_ref.shape[1]
    w = A_WIDTH
    rows = nbat * tt
    nsub = rows // SUB
    nsub_b = tt // SUB

    @pl.when(pl.program_id(1) == 0)
    def _():
        st_ref[...] = jnp.zeros_like(st_ref)

    logits = lbl_ref[...]
    ex = jnp.exp(logits - jnp.max(logits, axis=0, keepdims=True))
    p = ex / jnp.sum(ex, axis=0, keepdims=True)
    lb = jnp.clip(jnp.sum(p[0:layer + 1], axis=0, keepdims=True) - p[0:1], 0.0, 1.0 - EPS)

    def cols(j):
        return pa_ref[:, :, j * w:(j + 1) * w].reshape(rows, w)

    q = _silu(cols(0))
    fp = cols(1)
    v = cols(2)
    f = lb + (1.0 - lb) * _sigmoid(fp)
    logf = jnp.log(jnp.maximum(f, TINY))
    k = (1.0 - lb) * _sigmoid(-fp)

    ones_bd = ones_ref[...]
    tri = tri_ref[...]
    cum = jnp.concatenate([_mask_dot(tri, logf[b * tt:(b + 1) * tt]) for b in range(nbat)], axis=0)

    q3 = q.reshape(nsub, SUB, w)
    k3 = k.reshape(nsub, SUB, w)
    v3 = v.reshape(nsub, SUB, w)
    c3 = cum.reshape(nsub, SUB, w)
    row = lax.broadcasted_iota(jnp.int32, (nsub, SUB, w), 1)

    o_intra = jnp.zeros((nsub, SUB, w), F32)
    for j in range(SUB):
        keep = row >= j
        dec = jnp.where(keep, jnp.exp(jnp.where(keep, c3 - c3[:, j:j + 1, :], 0.0)), 0.0)
        x = (q3 * k3[:, j:j + 1, :] * dec).reshape(rows, w).astype(BF16)
        attn = _dot(x, ones_bd)
        o_intra = o_intra + attn.reshape(nsub, SUB, w) * v3[:, j:j + 1, :]

    c_last = c3[:, SUB - 1:SUB, :]
    qd = (q3 * jnp.exp(c3)).astype(BF16)
    kd = (k3 * jnp.exp(c_last - c3)).astype(BF16)
    v_bf = v3.astype(BF16)
    blk = (lax.broadcasted_iota(jnp.int32, (w, w), 0) // A_HEAD_DIM
           == lax.broadcasted_iota(jnp.int32, (w, w), 1) // A_HEAD_DIM)
    sts = [st_ref[b] for b in range(nbat)]
    o_parts = [[None] * nsub_b for _ in range(nbat)]
    for i in range(nsub_b):
        for b in range(nbat):
            n = b * nsub_b + i
            o_parts[b][i] = _dot_nt(qd[n], sts[b].astype(BF16)) + o_intra[n]
            upd = _dot_tn(v_bf[n], kd[n])
            sts[b] = sts[b] * jnp.exp(c_last[n]) + jnp.where(blk, upd, 0.0)
    for b in range(nbat):
        st_ref[b] = sts[b]
    o = jnp.concatenate([o_parts[b][i] for b in range(nbat) for i in range(nsub_b)], axis=0)

    ms = _dot_mask(o * o, ones_bd) * (1.0 / A_HEAD_DIM)
    o = o * lax.rsqrt(ms + EPS) * g_ref[...] * _silu(cols(3))
    o_ref[...] = o.reshape(nbat, tt, w).astype(o_ref.dtype)


def _hgrn2(pa, lb_logits, norm_g, layer, batch):
    t = pa.shape[0]
    s = t // batch
    tt = min(HGRN_TILE, s)
    nbat = min(MIX_BATCH, batch)
    w = A_WIDTH
    idx = np.arange(w)
    ones_bd = jnp.asarray(idx[:, None] // A_HEAD_DIM == idx[None, :] // A_HEAD_DIM, BF16)
    it = np.arange(tt)
    tri = jnp.asarray((it[:, None] // SUB == it[None, :] // SUB) & (it[:, None] >= it[None, :]), BF16)
    g = jnp.tile(norm_g.astype(F32), A_HEADS).reshape(1, w)
    tile = lambda b, i: (b, i, 0)
    const = lambda b, i: (0, 0)
    out = pl.pallas_call(
        functools.partial(_hgrn2_kernel, layer=layer),
        grid=(batch // nbat, s // tt),
        in_specs=[pl.BlockSpec((nbat, tt, PA_W), tile), pl.BlockSpec(lb_logits.shape, const),
                  pl.BlockSpec((1, w), const), pl.BlockSpec((w, w), const),
                  pl.BlockSpec((tt, tt), const)],
        out_specs=pl.BlockSpec((nbat, tt, w), tile),
        out_shape=jax.ShapeDtypeStruct((batch, s, w), BF16),
        scratch_shapes=[pltpu.VMEM((nbat, w, w), F32)],
        compiler_params=pltpu.CompilerParams(dimension_semantics=("arbitrary", "arbitrary"),
                                             vmem_limit_bytes=VMEM_LIMIT),
        name="hgrn2",
    )(pa.reshape(batch, s, PA_W), lb_logits.astype(F32), g, ones_bd, tri)
    return out.reshape(t, w)


def _gdn_kernel(pb_ref, pba_ref, cw_ref, alog_ref, dtb_ref, g_ref, tri_ref, o_ref,
                xbuf_ref, s_ref):
    nbat, tt = pb_ref.shape[0], pb_ref.shape[1]
    hd = B_HEAD_DIM
    rows = nbat * tt
    nchunk = tt // CHUNK
    nct = nbat * nchunk
    qkv_w = 3 * B_WIDTH
    pad = SUBLANES

    @pl.when(pl.program_id(1) == 0)
    def _():
        s_ref[...] = jnp.zeros_like(s_ref)
        xbuf_ref[:, 0:pad, :] = jnp.zeros((nbat, pad, qkv_w), F32)

    xbuf_ref[:, pad:pad + tt, :] = pb_ref[:, :, 0:qkv_w]
    y = xbuf_ref[:, pad - CONV_K + 1:pad - CONV_K + 1 + tt, :] * cw_ref[0:1, :]
    for j in range(1, CONV_K):
        y = y + xbuf_ref[:, pad - CONV_K + 1 + j:pad - CONV_K + 1 + j + tt, :] * cw_ref[j:j + 1, :]
    xbuf_ref[:, 0:pad, :] = xbuf_ref[:, tt:tt + pad, :]
    qkv = _silu(y).reshape(rows, qkv_w)

    lane = lax.broadcasted_iota(jnp.int32, (1, LANES), 1)
    is_alpha = (lane >= B_HEADS) & (lane < 2 * B_HEADS)
    ba = pba_ref[...].reshape(rows, LANES)
    beta_all = _sigmoid(ba)
    rate = jnp.where(is_alpha, jnp.exp(alog_ref[...]), 0.0)
    log_alpha = -rate * _softplus(ba + dtb_ref[...])
    tri = tri_ref[...]
    gc_all = jnp.concatenate([_mask_dot(tri, log_alpha[b * tt:(b + 1) * tt]) for b in range(nbat)], axis=0)
    gc_t = gc_all.T

    r_i = lax.broadcasted_iota(jnp.int32, (CHUNK, CHUNK), 0)
    c_i = lax.broadcasted_iota(jnp.int32, (CHUNK, CHUNK), 1)
    causal = r_i >= c_i
    strict = r_i > c_i
    eye = (r_i == c_i).astype(F32)
    scale = hd ** -0.5

    def chunks(xs):
        return jnp.concatenate([x.reshape(nct, CHUNK, x.shape[-1]) for x in xs], axis=0)

    q_l, k_l, v_l, beta_l, g_l, grow_l = [], [], [], [], [], []
    for h in range(B_HEADS):
        qh = qkv[:, h * hd:(h + 1) * hd]
        kh = qkv[:, B_WIDTH + h * hd:B_WIDTH + (h + 1) * hd]
        q_l.append(qh * lax.rsqrt(jnp.sum(qh * qh, axis=-1, keepdims=True) + EPS))
        k_l.append(kh * lax.rsqrt(jnp.sum(kh * kh, axis=-1, keepdims=True) + EPS))
        v_l.append(qkv[:, 2 * B_WIDTH + h * hd:2 * B_WIDTH + (h + 1) * hd])
        beta_l.append(jnp.broadcast_to(beta_all[:, h:h + 1], (rows, hd)))
        g_l.append(jnp.broadcast_to(gc_all[:, B_HEADS + h:B_HEADS + h + 1], (rows, hd)))
        for c in range(nct):
            grow_l.append(gc_t[B_HEADS + h:B_HEADS + h + 1, c * CHUNK:(c + 1) * CHUNK][None])
    q3, k3, v3, beta3, g3 = chunks(q_l), chunks(k_l), chunks(v_l), chunks(beta_l), chunks(g_l)
    g_row3 = jnp.concatenate(grow_l, axis=0)
    eg3 = jnp.exp(g3)
    g_last3 = g3[:, CHUNK - 1:CHUNK, :]
    kb3 = k3 * beta3
    k_bf = k3.astype(BF16)
    decay = jnp.where(causal, jnp.exp(jnp.where(causal, g3[:, :, 0:CHUNK] - g_row3, 0.0)), 0.0)
    kq = _bdot_nt(jnp.concatenate([kb3, q3], axis=1).astype(BF16), k_bf)
    a_mat = jnp.where(strict, kq[:, 0:CHUNK] * decay, 0.0)
    qk = kq[:, CHUNK:2 * CHUNK] * scale * decay
    t_inv = eye - jnp.where(r_i // 2 == c_i // 2, a_mat, 0.0)
    b = 2
    while b < CHUNK:
        off = jnp.where((r_i // (2 * b) == c_i // (2 * b)) & (r_i // b != c_i // b), a_mat, 0.0)
        t_bf = t_inv.astype(BF16)
        t_inv = t_inv - _bdot(t_bf, _bdot(off.astype(BF16), t_bf).astype(BF16))
        b *= 2
    rhs = jnp.concatenate([v3 * beta3, kb3 * eg3], axis=2).astype(BF16)
    uw = _bdot(t_inv.astype(BF16), rhs)
    qd3 = q3 * (scale * eg3)
    kd3 = k3 * jnp.exp(g_last3 - g3)
    sdec3 = jnp.exp(g_last3)

    states = [[s_ref[bb * B_HEADS + h] for h in range(B_HEADS)] for bb in range(nbat)]
    o_l = [[[None] * nchunk for _ in range(B_HEADS)] for _ in range(nbat)]
    for c in range(nchunk):
        for bb in range(nbat):
            for h in range(B_HEADS):
                i = h * nct + bb * nchunk + c
                s_cur = states[bb][h]
                m1 = _dot(jnp.concatenate([uw[i, :, hd:2 * hd], qd3[i]], axis=0).astype(BF16),
                          s_cur.astype(BF16))
                v_new = uw[i, :, 0:hd] - m1[0:CHUNK]
                m2 = _dot(jnp.concatenate([qk[i], kd3[i].T], axis=0).astype(BF16), v_new.astype(BF16))
                o_l[bb][h][c] = m1[CHUNK:2 * CHUNK] + m2[0:CHUNK]
                states[bb][h] = s_cur * sdec3[i] + m2[CHUNK:CHUNK + hd]
    for bb in range(nbat):
        for h in range(B_HEADS):
            s_ref[bb * B_HEADS + h] = states[bb][h]
            o = jnp.concatenate(o_l[bb][h], axis=0) if nchunk > 1 else o_l[bb][h][0]
            z = pb_ref[bb, :, qkv_w + h * hd:qkv_w + (h + 1) * hd]
            o_ref[bb, :, h * hd:(h + 1) * hd] = (_rms_scale(o) * g_ref[...] * _silu(z)).astype(o_ref.dtype)


def _gdn(pb, pba, conv_w, a_log, dt_bias, norm_g, batch):
    t = pb.shape[0]
    s = t // batch
    tt = min(GDN_TILE, s)
    nbat = min(MIX_BATCH, batch)
    it = np.arange(tt)
    tri = jnp.asarray((it[:, None] // CHUNK == it[None, :] // CHUNK) & (it[:, None] >= it[None, :]), BF16)
    alog = jnp.zeros((1, LANES), F32).at[0, B_HEADS:2 * B_HEADS].set(a_log.astype(F32))
    dtb = jnp.zeros((1, LANES), F32).at[0, B_HEADS:2 * B_HEADS].set(dt_bias.astype(F32))
    g = norm_g.astype(F32).reshape(1, B_HEAD_DIM)
    tile = lambda b, i: (b, i, 0)
    const = lambda b, i: (0, 0)
    out = pl.pallas_call(
        _gdn_kernel,
        grid=(batch // nbat, s // tt),
        in_specs=[pl.BlockSpec((nbat, tt, PB_W), tile), pl.BlockSpec((nbat, tt, PBA_W), tile),
                  pl.BlockSpec(conv_w.shape, const), pl.BlockSpec((1, LANES), const),
                  pl.BlockSpec((1, LANES), const), pl.BlockSpec((1, B_HEAD_DIM), const),
                  pl.BlockSpec((tt, tt), const)],
        out_specs=pl.BlockSpec((nbat, tt, B_WIDTH), tile),
        out_shape=jax.ShapeDtypeStruct((batch, s, B_WIDTH), BF16),
        scratch_shapes=[pltpu.VMEM((nbat, tt + SUBLANES, 3 * B_WIDTH), F32),
                        pltpu.VMEM((nbat * B_HEADS, B_HEAD_DIM, B_HEAD_DIM), F32)],
        compiler_params=pltpu.CompilerParams(dimension_semantics=("arbitrary", "arbitrary"),
                                             vmem_limit_bytes=VMEM_LIMIT),
        name="gdn",
    )(pb.reshape(batch, s, PB_W), pba.reshape(batch, s, PBA_W), conv_w.astype(F32), alog, dtb, g, tri)
    return out.reshape(t, B_WIDTH)


def _rglru_kernel(pc_ref, cw_ref, cb_ref, wa_ref, ba_ref, wx_ref, bx_ref, lam_ref, o_ref,
                  xbuf_ref, h_ref):
    tt = pc_ref.shape[0]
    w = C_WIDTH
    pad = SUBLANES
    first = pl.program_id(1) == 0

    @pl.when(first)
    def _():
        h_ref[...] = jnp.zeros_like(h_ref)
        xbuf_ref[0:pad, :] = jnp.zeros((pad, w), F32)

    xbuf_ref[pad:pad + tt, :] = pc_ref[:, 0:w]
    xc = xbuf_ref[pad - CONV_K + 1:pad - CONV_K + 1 + tt, :] * cw_ref[0:1, :]
    for j in range(1, CONV_K):
        xc = xc + xbuf_ref[pad - CONV_K + 1 + j:pad - CONV_K + 1 + j + tt, :] * cw_ref[j:j + 1, :]
    xbuf_ref[0:pad, :] = xbuf_ref[tt:tt + pad, :]
    xc = xc + cb_ref[...]

    x_bf = xc.astype(BF16)
    r = _sigmoid(_dot(x_bf, wa_ref[...]) + ba_ref[...])
    gate_i = _sigmoid(_dot(x_bf, wx_ref[...]) + bx_ref[...])
    log_a = -RG_C * r * _softplus(-lam_ref[...])
    a = jnp.exp(log_a)
    row = lax.broadcasted_iota(jnp.int32, (tt, w), 0)
    mult = jnp.sqrt(jnp.maximum(-jnp.tanh(log_a) * (a * a + 1.0), EPS))
    mult = jnp.where(row + pl.program_id(1) * tt == 0, 1.0, mult)
    b = mult * gate_i * xc

    d = 1
    while d < tt:
        keep = row >= d
        a_prev = jnp.where(keep, pltpu.roll(a, d, axis=0), 1.0)
        b_prev = jnp.where(keep, pltpu.roll(b, d, axis=0), 0.0)
        b = a * b_prev + b
        a = a * a_prev
        d *= 2
    h = b + a * h_ref[0:1, :]
    h_ref[...] = jnp.broadcast_to(h[tt - 1:tt, :], h_ref.shape)

    y = pc_ref[:, w:2 * w]
    gelu = 0.5 * y * (1.0 + jnp.tanh(0.7978845608028654 * (y + 0.044715 * (y * y * y))))
    o_ref[...] = (gelu * h).astype(o_ref.dtype)


def _block_diag(wb):
    n, d, _ = wb.shape
    out = jnp.zeros((n * d, n * d), wb.dtype)
    for i in range(n):
        out = out.at[i * d:(i + 1) * d, i * d:(i + 1) * d].set(wb[i])
    return out


def _rglru(pc, conv_w, conv_b, w_a, b_a, w_x, b_x, lam, batch):
    t = pc.shape[0]
    s = t // batch
    tt = min(MIX_TILE, s)
    nt = s // tt
    w = C_WIDTH
    vec = lambda a: a.astype(F32).reshape(1, w)
    row = lambda b, i: (b * nt + i, 0)
    const = lambda b, i: (0, 0)
    vspec = pl.BlockSpec((1, w), const)
    mspec = pl.BlockSpec((w, w), const)
    return pl.pallas_call(
        _rglru_kernel,
        grid=(batch, nt),
        in_specs=[pl.BlockSpec((tt, PC_W), row), pl.BlockSpec(conv_w.shape, const), vspec,
                  mspec, vspec, mspec, vspec, vspec],
        out_specs=pl.BlockSpec((tt, w), row),
        out_shape=jax.ShapeDtypeStruct((t, w), BF16),
        scratch_shapes=[pltpu.VMEM((tt + SUBLANES, w), F32), pltpu.VMEM((SUBLANES, w), F32)],
        compiler_params=pltpu.CompilerParams(dimension_semantics=("arbitrary", "arbitrary"),
                                             vmem_limit_bytes=VMEM_LIMIT),
        name="rglru",
    )(pc, conv_w.astype(F32), vec(conv_b), _block_diag(w_a).astype(BF16), vec(b_a),
      _block_diag(w_x).astype(BF16), vec(b_x), vec(lam))


def _reorder_w_in(w):
    a_end = PA_W
    b_end = a_end + PB_W
    ba_end = b_end + 2 * B_HEADS
    pad = jnp.zeros((w.shape[0], PBA_W - 2 * B_HEADS), w.dtype)
    return jnp.concatenate([w[:, :b_end], w[:, ba_end:], w[:, b_end:ba_end], pad], axis=1).astype(BF16)


def kernel(x, norm1_g, w_in, hgrn_lb_logits, hgrn_norm_g, gdn_conv_w, gdn_a_log, gdn_dt_bias, gdn_norm_g, lru_conv_w, lru_conv_b, lru_w_a, lru_b_a, lru_w_x, lru_b_x, lru_lambda, w_out, norm2_g, w_up, w_down, final_norm_g):
    bsz, seq, d = x.shape
    depth = w_in.shape[0]
    xf = x.reshape(bsz * seq, d).astype(F32)
    gf = final_norm_g.astype(F32).reshape(1, d)
    for l in range(depth):
        pa, pb, pc, pba = _inproj(xf, norm1_g[l].astype(F32).reshape(1, d), _reorder_w_in(w_in[l]))
        oa = _hgrn2(pa, hgrn_lb_logits, hgrn_norm_g[l], l, bsz)
        ob = _gdn(pb, pba, gdn_conv_w[l], gdn_a_log[l], gdn_dt_bias[l], gdn_norm_g[l], bsz)
        oc = _rglru(pc, lru_conv_w[l], lru_conv_b[l], lru_w_a[l], lru_b_a[l], lru_w_x[l], lru_b_x[l],
                    lru_lambda[l], bsz)
        xf = _outmlp(xf, oa, ob, oc, w_out[l].astype(BF16), norm2_g[l].astype(F32).reshape(1, d),
                     w_up[l].astype(BF16), w_down[l].astype(BF16), gf, l == depth - 1)
    return xf.reshape(bsz, seq, d).astype(x.dtype)
```

```python
import functools

import jax
import jax.numpy as jnp
import numpy as np
from jax import lax
from jax.experimental import pallas as pl
from jax.experimental.pallas import tpu as pltpu

F32 = jnp.float32
BF16 = jnp.bfloat16

A_HEADS = 4
A_HEAD_DIM = 64
A_WIDTH = A_HEADS * A_HEAD_DIM
B_HEADS = 4
B_HEAD_DIM = 128
B_WIDTH = B_HEADS * B_HEAD_DIM
C_BLOCKS = 4
C_WIDTH = 256
C_BLOCK_DIM = C_WIDTH // C_BLOCKS
CONV_K = 4
CHUNK = 64
SUB = 16
RG_C = 8.0
EPS = 1e-6
TINY = 1e-30

LANES = 128
SUBLANES = 8
VMEM_LIMIT = 56 * 1024 * 1024

PROJ_TILE = 512
MIX_TILE = 256
MIX_BATCH = 4
HGRN_TILE = 128
GDN_TILE = 128
FF_CHUNK = 1024

PA_W = 4 * A_WIDTH
PB_W = 4 * B_WIDTH
PC_W = 2 * C_WIDTH
PBA_W = LANES


def _sigmoid(x):
    return 1.0 / (1.0 + jnp.exp(-x))


def _silu(x):
    h = 0.5 * x
    return h * (1.0 + jnp.tanh(h))


def _softplus(x):
    return jnp.maximum(x, 0.0) + jnp.log1p(jnp.exp(-jnp.abs(x)))


def _split3(x):
    hi = x.astype(BF16)
    r1 = x - hi.astype(F32)
    mid = r1.astype(BF16)
    lo = (r1 - mid.astype(F32)).astype(BF16)
    return hi, mid, lo


def _dot(a, b):
    return jnp.dot(a, b, preferred_element_type=F32)


def _dot_nt(a, b):
    return lax.dot_general(a, b, (((1,), (1,)), ((), ())), preferred_element_type=F32)


def _dot_tn(a, b):
    return lax.dot_general(a, b, (((0,), (0,)), ((), ())), preferred_element_type=F32)


def _bdot(a, b):
    return lax.dot_general(a, b, (((2,), (1,)), ((0,), (0,))), preferred_element_type=F32)


def _bdot_nt(a, b):
    return lax.dot_general(a, b, (((2,), (2,)), ((0,), (0,))), preferred_element_type=F32)


def _mask_dot(m01, x):
    hi, mid, lo = _split3(x)
    return _dot(m01, hi) + _dot(m01, mid) + _dot(m01, lo)


def _dot_mask(x, m01):
    hi, mid, lo = _split3(x)
    return _dot(hi, m01) + _dot(mid, m01) + _dot(lo, m01)


def _causal_conv(x, carry_ref, b, w_ref):
    tt = x.shape[0]
    pad = SUBLANES
    ext = jnp.concatenate([carry_ref[b], x], axis=0)
    y = x * w_ref[CONV_K - 1:CONV_K, :]
    for s in range(1, CONV_K):
        y = y + pltpu.roll(ext, s, axis=0)[pad:pad + tt] * w_ref[CONV_K - 1 - s:CONV_K - s, :]
    carry_ref[b] = x[tt - pad:tt]
    return y


def _rms_scale(x):
    return x * lax.rsqrt(jnp.mean(x * x, axis=-1, keepdims=True) + EPS)


def _inproj_kernel(x_ref, g_ref, w_ref, pa_ref, pb_ref, pc_ref, pba_ref):
    h = (_rms_scale(x_ref[...]) * g_ref[...]).astype(BF16)
    off = 0
    for ref, width in ((pa_ref, PA_W), (pb_ref, PB_W), (pc_ref, PC_W), (pba_ref, PBA_W)):
        ref[...] = _dot(h, w_ref[:, off:off + width])
        off += width


def _inproj(xf, g, w):
    t, d = xf.shape
    n = w.shape[1]
    tm = min(PROJ_TILE, t)
    row = lambda i: (i, 0)
    const = lambda i: (0, 0)
    return pl.pallas_call(
        _inproj_kernel,
        grid=(t // tm,),
        in_specs=[pl.BlockSpec((tm, d), row), pl.BlockSpec((1, d), const),
                  pl.BlockSpec((d, n), const)],
        out_specs=[pl.BlockSpec((tm, PA_W), row), pl.BlockSpec((tm, PB_W), row),
                   pl.BlockSpec((tm, PC_W), row), pl.BlockSpec((tm, PBA_W), row)],
        out_shape=[jax.ShapeDtypeStruct((t, PA_W), F32), jax.ShapeDtypeStruct((t, PB_W), F32),
                   jax.ShapeDtypeStruct((t, PC_W), F32), jax.ShapeDtypeStruct((t, PBA_W), F32)],
        compiler_params=pltpu.CompilerParams(dimension_semantics=("arbitrary",),
                                             vmem_limit_bytes=VMEM_LIMIT),
        name="inproj",
    )(xf, g, w)


def _outmlp_kernel(x_ref, oa_ref, ob_ref, oc_ref, wo_ref, g2_ref, wup_ref, wdn_ref, gf_ref,
                   o_ref, *, final):
    x1 = x_ref[...]
    x1 = x1 + _dot(oa_ref[...], wo_ref[0:A_WIDTH, :])
    x1 = x1 + _dot(ob_ref[...], wo_ref[A_WIDTH:A_WIDTH + B_WIDTH, :])
    x1 = x1 + _dot(oc_ref[...], wo_ref[A_WIDTH + B_WIDTH:, :])
    o_ref[...] = x1
    h2 = (_rms_scale(x1) * g2_ref[...]).astype(BF16)
    d_ff = wup_ref.shape[1]
    for c in range(d_ff // FF_CHUNK):
        u = jnp.maximum(_dot(h2, wup_ref[:, c * FF_CHUNK:(c + 1) * FF_CHUNK]), 0.0)
        o_ref[...] += _dot((u * u).astype(BF16), wdn_ref[c * FF_CHUNK:(c + 1) * FF_CHUNK, :])
    if final:
        o_ref[...] = _rms_scale(o_ref[...]) * gf_ref[...]


def _outmlp(xf, oa, ob, oc, wo, g2, wup, wdn, gf, final):
    t, d = xf.shape
    tm = min(PROJ_TILE, t)
    row = lambda i: (i, 0)
    const = lambda i: (0, 0)
    full = lambda a: pl.BlockSpec(a.shape, const)
    return pl.pallas_call(
        functools.partial(_outmlp_kernel, final=final),
        grid=(t // tm,),
        in_specs=[pl.BlockSpec((tm, d), row), pl.BlockSpec((tm, A_WIDTH), row),
                  pl.BlockSpec((tm, B_WIDTH), row), pl.BlockSpec((tm, C_WIDTH), row),
                  full(wo), full(g2), full(wup), full(wdn), full(gf)],
        out_specs=pl.BlockSpec((tm, d), row),
        out_shape=jax.ShapeDtypeStruct((t, d), F32),
        compiler_params=pltpu.CompilerParams(dimension_semantics=("arbitrary",),
                                             vmem_limit_bytes=VMEM_LIMIT),
        name="outmlp",
    )(xf, oa, ob, oc, wo, g2, wup, wdn, gf)


def _hgrn2_kernel(pa_ref, lbl_ref, g_ref, ones_ref, tri_ref, o_ref, st_ref, *, layer):
    nbat, tt = pa_ref.shape[0], pa_ref.shape[1]
    w = A_WIDTH
    rows = nbat * tt
    nsub = rows // SUB
    nsub_b = tt // SUB

    @pl.when(pl.program_id(1) == 0)
    def _():
        st_ref[...] = jnp.zeros_like(st_ref)

    logits = lbl_ref[...]
    ex = jnp.exp(logits - jnp.max(logits, axis=0, keepdims=True))
    p = ex / jnp.sum(ex, axis=0, keepdims=True)
    lb = jnp.clip(jnp.sum(p[0:layer + 1], axis=0, keepdims=True) - p[0:1], 0.0, 1.0 - EPS)

    def cols(j):
        return pa_ref[:, :, j * w:(j + 1) * w].reshape(rows, w)

    q = _silu(cols(0))
    fp = cols(1)
    v = cols(2)
    f = lb + (1.0 - lb) * _sigmoid(fp)
    logf = jnp.log(jnp.maximum(f, TINY))
    k = (1.0 - lb) * _sigmoid(-fp)

    ones_bd = ones_ref[...]
    tri = tri_ref[...]
    cum = jnp.concatenate([_mask_dot(tri, logf[b * tt:(b + 1) * tt]) for b in range(nbat)], axis=0)

    q3 = q.reshape(nsub, SUB, w)
    k3 = k.reshape(nsub, SUB, w)
    v3 = v.reshape(nsub, SUB, w)
    c3 = cum.reshape(nsub, SUB, w)
    row = lax.broadcasted_iota(jnp.int32, (nsub, SUB, w), 1)

    o_intra = jnp.zeros((nsub, SUB, w), F32)
    for j in range(SUB):
        keep = row >= j
        dec = jnp.where(keep, jnp.exp(c3 - c3[:, j:j + 1, :]), 0.0)
        x = (q3 * k3[:, j:j + 1, :] * dec).reshape(rows, w).astype(BF16)
        attn = _dot(x, ones_bd)
        o_intra = o_intra + attn.reshape(nsub, SUB, w) * v3[:, j:j + 1, :]

    c_last = c3[:, SUB - 1:SUB, :]
    qd = (q3 * jnp.exp(c3)).astype(BF16)
    kd = (k3 * jnp.exp(c_last - c3)).astype(BF16)
    v_bf = v3.astype(BF16)
    blk = (lax.broadcasted_iota(jnp.int32, (w, w), 0) // A_HEAD_DIM
           == lax.broadcasted_iota(jnp.int32, (w, w), 1) // A_HEAD_DIM)
    sts = [st_ref[b] for b in range(nbat)]
    o_parts = [[None] * nsub_b for _ in range(nbat)]
    for i in range(nsub_b):
        for b in range(nbat):
            n = b * nsub_b + i
            o_parts[b][i] = _dot_nt(qd[n], sts[b].astype(BF16)) + o_intra[n]
            upd = _dot_tn(v_bf[n], kd[n])
            sts[b] = sts[b] * jnp.exp(c_last[n]) + jnp.where(blk, upd, 0.0)
    for b in range(nbat):
        st_ref[b] = sts[b]
    o = jnp.concatenate([o_parts[b][i] for b in range(nbat) for i in range(nsub_b)], axis=0)

    ms = _dot_mask(o * o, ones_bd) * (1.0 / A_HEAD_DIM)
    o = o * lax.rsqrt(ms + EPS) * g_ref[...] * _silu(cols(3))
    o_ref[...] = o.reshape(nbat, tt, w).astype(o_ref.dtype)


def _hgrn2(pa, lb_logits, norm_g, layer, batch):
    t = pa.shape[0]
    s = t // batch
    tt = min(HGRN_TILE, s)
    nbat = min(MIX_BATCH, batch)
    w = A_WIDTH
    idx = np.arange(w)
    ones_bd = jnp.asarray(idx[:, None] // A_HEAD_DIM == idx[None, :] // A_HEAD_DIM, BF16)
    it = np.arange(tt)
    tri = jnp.asarray((it[:, None] // SUB == it[None, :] // SUB) & (it[:, None] >= it[None, :]), BF16)
    g = jnp.tile(norm_g.astype(F32), A_HEADS).reshape(1, w)
    tile = lambda b, i: (b, i, 0)
    const = lambda b, i: (0, 0)
    out = pl.pallas_call(
        functools.partial(_hgrn2_kernel, layer=layer),
        grid=(batch // nbat, s // tt),
        in_specs=[pl.BlockSpec((nbat, tt, PA_W), tile), pl.BlockSpec(lb_logits.shape, const),
                  pl.BlockSpec((1, w), const), pl.BlockSpec((w, w), const),
                  pl.BlockSpec((tt, tt), const)],
        out_specs=pl.BlockSpec((nbat, tt, w), tile),
        out_shape=jax.ShapeDtypeStruct((batch, s, w), BF16),
        scratch_shapes=[pltpu.VMEM((nbat, w, w), F32)],
        compiler_params=pltpu.CompilerParams(dimension_semantics=("arbitrary", "arbitrary"),
                                             vmem_limit_bytes=VMEM_LIMIT),
        name="hgrn2",
    )(pa.reshape(batch, s, PA_W), lb_logits.astype(F32), g, ones_bd, tri)
    return out.reshape(t, w)


def _gdn_kernel(pb_ref, pba_ref, cw_ref, alog_ref, dtb_ref, g_ref, tri_ref, o_ref,
                xbuf_ref, s_ref):
    nbat, tt = pb_ref.shape[0], pb_ref.shape[1]
    hd = B_HEAD_DIM
    rows = nbat * tt
    nchunk = tt // CHUNK
    nct = nbat * nchunk
    qkv_w = 3 * B_WIDTH

    @pl.when(pl.program_id(1) == 0)
    def _():
        s_ref[...] = jnp.zeros_like(s_ref)
        xbuf_ref[...] = jnp.zeros_like(xbuf_ref)

    qkv = _silu(jnp.concatenate(
        [_causal_conv(pb_ref[b, :, 0:qkv_w], xbuf_ref, b, cw_ref) for b in range(nbat)], axis=0))

    lane = lax.broadcasted_iota(jnp.int32, (1, LANES), 1)
    is_alpha = (lane >= B_HEADS) & (lane < 2 * B_HEADS)
    ba = pba_ref[...].reshape(rows, LANES)
    beta_all = _sigmoid(ba)
    rate = jnp.where(is_alpha, jnp.exp(alog_ref[...]), 0.0)
    log_alpha = -rate * _softplus(ba + dtb_ref[...])
    tri = tri_ref[...]
    gc_all = jnp.concatenate([_mask_dot(tri, log_alpha[b * tt:(b + 1) * tt]) for b in range(nbat)], axis=0)
    gc_t = gc_all.T

    r_i = lax.broadcasted_iota(jnp.int32, (CHUNK, CHUNK), 0)
    c_i = lax.broadcasted_iota(jnp.int32, (CHUNK, CHUNK), 1)
    causal = r_i >= c_i
    strict = r_i > c_i
    eye = (r_i == c_i).astype(F32)
    scale = hd ** -0.5

    def chunks(xs):
        return jnp.concatenate([x.reshape(nct, CHUNK, x.shape[-1]) for x in xs], axis=0)

    q_l, k_l, v_l, beta_l, g_l, grow_l = [], [], [], [], [], []
    for h in range(B_HEADS):
        qh = qkv[:, h * hd:(h + 1) * hd]
        kh = qkv[:, B_WIDTH + h * hd:B_WIDTH + (h + 1) * hd]
        q_l.append(qh * lax.rsqrt(jnp.sum(qh * qh, axis=-1, keepdims=True) + EPS))
        k_l.append(kh * lax.rsqrt(jnp.sum(kh * kh, axis=-1, keepdims=True) + EPS))
        v_l.append(qkv[:, 2 * B_WIDTH + h * hd:2 * B_WIDTH + (h + 1) * hd])
        beta_l.append(jnp.broadcast_to(beta_all[:, h:h + 1], (rows, hd)))
        g_l.append(jnp.broadcast_to(gc_all[:, B_HEADS + h:B_HEADS + h + 1], (rows, hd)))
        for c in range(nct):
            grow_l.append(gc_t[B_HEADS + h:B_HEADS + h + 1, c * CHUNK:(c + 1) * CHUNK][None])
    q3, k3, v3, beta3, g3 = chunks(q_l), chunks(k_l), chunks(v_l), chunks(beta_l), chunks(g_l)
    g_row3 = jnp.concatenate(grow_l, axis=0)
    eg3 = jnp.exp(g3)
    g_last3 = g3[:, CHUNK - 1:CHUNK, :]
    kb3 = k3 * beta3
    k_bf = k3.astype(BF16)
    decay = jnp.where(causal, jnp.exp(jnp.where(causal, g3[:, :, 0:CHUNK] - g_row3, 0.0)), 0.0)
    kq = _bdot_nt(jnp.concatenate([kb3, q3], axis=1).astype(BF16), k_bf)
    a_mat = jnp.where(strict, kq[:, 0:CHUNK] * decay, 0.0)
    qk = kq[:, CHUNK:2 * CHUNK] * scale * decay
    t_inv = eye - jnp.where(r_i // 2 == c_i // 2, a_mat, 0.0)
    b = 2
    while b < CHUNK:
        off = jnp.where((r_i // (2 * b) == c_i // (2 * b)) & (r_i // b != c_i // b), a_mat, 0.0)
        t_bf = t_inv.astype(BF16)
        t_inv = t_inv - _bdot(t_bf, _bdot(off.astype(BF16), t_bf).astype(BF16))
        b *= 2
    rhs = jnp.concatenate([v3 * beta3, kb3 * eg3], axis=2).astype(BF16)
    uw = _bdot(t_inv.astype(BF16), rhs)
    qd3 = q3 * (scale * eg3)
    kd3 = k3 * jnp.exp(g_last3 - g3)
    sdec3 = jnp.exp(g_last3)

    states = [[s_ref[bb * B_HEADS + h] for h in range(B_HEADS)] for bb in range(nbat)]
    o_l = [[[None] * nchunk for _ in range(B_HEADS)] for _ in range(nbat)]
    for c in range(nchunk):
        for bb in range(nbat):
            for h in range(B_HEADS):
                i = h * nct + bb * nchunk + c
                s_cur = states[bb][h]
                m1 = _dot(jnp.concatenate([uw[i, :, hd:2 * hd], qd3[i]], axis=0).astype(BF16),
                          s_cur.astype(BF16))
                v_new = uw[i, :, 0:hd] - m1[0:CHUNK]
                m2 = _dot(jnp.concatenate([qk[i], kd3[i].T], axis=0).astype(BF16), v_new.astype(BF16))
                o_l[bb][h][c] = m1[CHUNK:2 * CHUNK] + m2[0:CHUNK]
                states[bb][h] = s_cur * sdec3[i] + m2[CHUNK:CHUNK + hd]
    for bb in range(nbat):
        for h in range(B_HEADS):
            s_ref[bb * B_HEADS + h] = states[bb][h]
            o = jnp.concatenate(o_l[bb][h], axis=0) if nchunk > 1 else o_l[bb][h][0]
            z = pb_ref[bb, :, qkv_w + h * hd:qkv_w + (h + 1) * hd]
            o_ref[bb, :, h * hd:(h + 1) * hd] = (_rms_scale(o) * g_ref[...] * _silu(z)).astype(o_ref.dtype)


def _gdn(pb, pba, conv_w, a_log, dt_bias, norm_g, batch):
    t = pb.shape[0]
    s = t // batch
    tt = min(GDN_TILE, s)
    nbat = min(MIX_BATCH, batch)
    it = np.arange(tt)
    tri = jnp.asarray((it[:, None] // CHUNK == it[None, :] // CHUNK) & (it[:, None] >= it[None, :]), BF16)
    alog = jnp.zeros((1, LANES), F32).at[0, B_HEADS:2 * B_HEADS].set(a_log.astype(F32))
    dtb = jnp.zeros((1, LANES), F32).at[0, B_HEADS:2 * B_HEADS].set(dt_bias.astype(F32))
    g = norm_g.astype(F32).reshape(1, B_HEAD_DIM)
    tile = lambda b, i: (b, i, 0)
    const = lambda b, i: (0, 0)
    out = pl.pallas_call(
        _gdn_kernel,
        grid=(batch // nbat, s // tt),
        in_specs=[pl.BlockSpec((nbat, tt, PB_W), tile), pl.BlockSpec((nbat, tt, PBA_W), tile),
                  pl.BlockSpec(conv_w.shape, const), pl.BlockSpec((1, LANES), const),
                  pl.BlockSpec((1, LANES), const), pl.BlockSpec((1, B_HEAD_DIM), const),
                  pl.BlockSpec((tt, tt), const)],
        out_specs=pl.BlockSpec((nbat, tt, B_WIDTH), tile),
        out_shape=jax.ShapeDtypeStruct((batch, s, B_WIDTH), BF16),
        scratch_shapes=[pltpu.VMEM((nbat, SUBLANES, 3 * B_WIDTH), F32),
                        pltpu.VMEM((nbat * B_HEADS, B_HEAD_DIM, B_HEAD_DIM), F32)],
        compiler_params=pltpu.CompilerParams(dimension_semantics=("arbitrary", "arbitrary"),
                                             vmem_limit_bytes=VMEM_LIMIT),
        name="gdn",
    )(pb.reshape(batch, s, PB_W), pba.reshape(batch, s, PBA_W), conv_w.astype(F32), alog, dtb, g, tri)
    return out.reshape(t, B_WIDTH)


def _rglru_kernel(pc_ref, cw_ref, cb_ref, wa_ref, ba_ref, wx_ref, bx_ref, lam_ref, o_ref,
                  xbuf_ref, h_ref):
    tt = pc_ref.shape[0]
    w = C_WIDTH
    first = pl.program_id(1) == 0

    @pl.when(first)
    def _():
        h_ref[...] = jnp.zeros_like(h_ref)
        xbuf_ref[...] = jnp.zeros_like(xbuf_ref)

    xc = _causal_conv(pc_ref[:, 0:w], xbuf_ref, 0, cw_ref) + cb_ref[...]

    x_bf = xc.astype(BF16)
    r = _sigmoid(_dot(x_bf, wa_ref[...]) + ba_ref[...])
    gate_i = _sigmoid(_dot(x_bf, wx_ref[...]) + bx_ref[...])
    log_a = -RG_C * r * _softplus(-lam_ref[...])
    a = jnp.exp(log_a)
    row = lax.broadcasted_iota(jnp.int32, (tt, w), 0)
    mult = jnp.sqrt(jnp.maximum(-jnp.tanh(log_a) * (a * a + 1.0), EPS))
    mult = jnp.where(row + pl.program_id(1) * tt == 0, 1.0, mult)
    b = mult * gate_i * xc

    d = 1
    while d < tt:
        keep = row >= d
        a_prev = jnp.where(keep, pltpu.roll(a, d, axis=0), 1.0)
        b_prev = jnp.where(keep, pltpu.roll(b, d, axis=0), 0.0)
        b = a * b_prev + b
        a = a * a_prev
        d *= 2
    h = b + a * h_ref[0:1, :]
    h_ref[...] = jnp.broadcast_to(h[tt - 1:tt, :], h_ref.shape)

    y = pc_ref[:, w:2 * w]
    gelu = 0.5 * y * (1.0 + jnp.tanh(0.7978845608028654 * (y + 0.044715 * (y * y * y))))
    o_ref[...] = (gelu * h).astype(o_ref.dtype)


def _block_diag(wb):
    n, d, _ = wb.shape
    out = jnp.zeros((n * d, n * d), wb.dtype)
    for i in range(n):
        out = out.at[i * d:(i + 1) * d, i * d:(i + 1) * d].set(wb[i])
    return out


def _rglru(pc, conv_w, conv_b, w_a, b_a, w_x, b_x, lam, batch):
    t = pc.shape[0]
    s = t // batch
    tt = min(MIX_TILE, s)
    nt = s // tt
    w = C_WIDTH
    vec = lambda a: a.astype(F32).reshape(1, w)
    row = lambda b, i: (b * nt + i, 0)
    const = lambda b, i: (0, 0)
    vspec = pl.BlockSpec((1, w), const)
    mspec = pl.BlockSpec((w, w), const)
    return pl.pallas_call(
        _rglru_kernel,
        grid=(batch, nt),
        in_specs=[pl.BlockSpec((tt, PC_W), row), pl.BlockSpec(conv_w.shape, const), vspec,
                  mspec, vspec, mspec, vspec, vspec],
        out_specs=pl.BlockSpec((tt, w), row),
        out_shape=jax.ShapeDtypeStruct((t, w), BF16),
        scratch_shapes=[pltpu.VMEM((1, SUBLANES, w), F32), pltpu.VMEM((SUBLANES, w), F32)],
        compiler_params=pltpu.CompilerParams(dimension_semantics=("arbitrary", "arbitrary"),
                                             vmem_limit_bytes=VMEM_LIMIT),
        name="rglru",
    )(pc, conv_w.astype(F32), vec(conv_b), _block_diag(w_a).astype(BF16), vec(b_a),
      _block_diag(w_x).astype(BF16), vec(b_x), vec(lam))


def _reorder_w_in(w):
    a_end = PA_W
    b_end = a_end + PB_W
    ba_end = b_end + 2 * B_HEADS
    pad = jnp.zeros((w.shape[0], PBA_W - 2 * B_HEADS), w.dtype)
    return jnp.concatenate([w[:, :b_end], w[:, ba_end:], w[:, b_end:ba_end], pad], axis=1).astype(BF16)


def kernel(x, norm1_g, w_in, hgrn_lb_logits, hgrn_norm_g, gdn_conv_w, gdn_a_log, gdn_dt_bias, gdn_norm_g, lru_conv_w, lru_conv_b, lru_w_a, lru_b_a, lru_w_x, lru_b_x, lru_lambda, w_out, norm2_g, w_up, w_down, final_norm_g):
    bsz, seq, d = x.shape
    depth = w_in.shape[0]
    xf = x.reshape(bsz * seq, d).astype(F32)
    gf = final_norm_g.astype(F32).reshape(1, d)
    for l in range(depth):
        pa, pb, pc, pba = _inproj(xf, norm1_g[l].astype(F32).reshape(1, d), _reorder_w_in(w_in[l]))
        oa = _hgrn2(pa, hgrn_lb_logits, hgrn_norm_g[l], l, bsz)
        ob = _gdn(pb, pba, gdn_conv_w[l], gdn_a_log[l], gdn_dt_bias[l], gdn_norm_g[l], bsz)
        oc = _rglru(pc, lru_conv_w[l], lru_conv_b[l], lru_w_a[l], lru_b_a[l], lru_w_x[l], lru_b_x[l],
                    lru_lambda[l], bsz)
        xf = _outmlp(xf, oa, ob, oc, w_out[l].astype(BF16), norm2_g[l].astype(F32).reshape(1, d),
                     w_up[l].astype(BF16), w_down[l].astype(BF16), gf, l == depth - 1)
    return xf.reshape(bsz, seq, d).astype(x.dtype)
```

```python
import functools

import jax
import jax.numpy as jnp
import numpy as np
from jax import lax
from jax.experimental import pallas as pl
from jax.experimental.pallas import tpu as pltpu

F32 = jnp.float32
BF16 = jnp.bfloat16

A_HEADS = 4
A_HEAD_DIM = 64
A_WIDTH = A_HEADS * A_HEAD_DIM
B_HEADS = 4
B_HEAD_DIM = 128
B_WIDTH = B_HEADS * B_HEAD_DIM
C_BLOCKS = 4
C_WIDTH = 256
C_BLOCK_DIM = C_WIDTH // C_BLOCKS
CONV_K = 4
CHUNK = 64
SUB = 16
RG_C = 8.0
EPS = 1e-6
TINY = 1e-30

LANES = 128
SUBLANES = 8
VMEM_LIMIT = 56 * 1024 * 1024

PROJ_TILE = 512
MIX_TILE = 256
MIX_BATCH = 4
HGRN_TILE = 128
GDN_TILE = 128
FF_CHUNK = 1024

PA_W = 4 * A_WIDTH
PB_W = 4 * B_WIDTH
PC_W = 2 * C_WIDTH
PBA_W = LANES


def _sigmoid(x):
    return 1.0 / (1.0 + jnp.exp(-x))


def _silu(x):
    h = 0.5 * x
    return h * (1.0 + jnp.tanh(h))


def _softplus(x):
    return jnp.maximum(x, 0.0) + jnp.log1p(jnp.exp(-jnp.abs(x)))


def _split3(x):
    hi = x.astype(BF16)
    r1 = x - hi.astype(F32)
    mid = r1.astype(BF16)
    lo = (r1 - mid.astype(F32)).astype(BF16)
    return hi, mid, lo


def _dot(a, b):
    return jnp.dot(a, b, preferred_element_type=F32)


def _dot_nt(a, b):
    return lax.dot_general(a, b, (((1,), (1,)), ((), ())), preferred_element_type=F32)


def _dot_tn(a, b):
    return lax.dot_general(a, b, (((0,), (0,)), ((), ())), preferred_element_type=F32)


def _bdot(a, b):
    return lax.dot_general(a, b, (((2,), (1,)), ((0,), (0,))), preferred_element_type=F32)


def _bdot_nt(a, b):
    return lax.dot_general(a, b, (((2,), (2,)), ((0,), (0,))), preferred_element_type=F32)


def _mask_dot(m01, x):
    hi, mid, lo = _split3(x)
    return _dot(m01, hi) + _dot(m01, mid) + _dot(m01, lo)


def _dot_mask(x, m01):
    hi, mid, lo = _split3(x)
    return _dot(hi, m01) + _dot(mid, m01) + _dot(lo, m01)


def _causal_conv(x, carry_ref, b, w_ref):
    tt = x.shape[0]
    pad = SUBLANES
    ext = jnp.concatenate([carry_ref[b], x], axis=0)
    y = x * w_ref[CONV_K - 1:CONV_K, :]
    for s in range(1, CONV_K):
        y = y + pltpu.roll(ext, s, axis=0)[pad:pad + tt] * w_ref[CONV_K - 1 - s:CONV_K - s, :]
    carry_ref[b] = x[tt - pad:tt]
    return y


def _rms_scale(x):
    return x * lax.rsqrt(jnp.mean(x * x, axis=-1, keepdims=True) + EPS)


def _inproj_kernel(x_ref, g_ref, w_ref, pa_ref, pb_ref, pc_ref, pba_ref):
    h = (_rms_scale(x_ref[...]) * g_ref[...]).astype(BF16)
    off = 0
    for ref, width in ((pa_ref, PA_W), (pb_ref, PB_W), (pc_ref, PC_W), (pba_ref, PBA_W)):
        ref[...] = _dot(h, w_ref[:, off:off + width])
        off += width


def _inproj(xf, g, w):
    t, d = xf.shape
    n = w.shape[1]
    tm = min(PROJ_TILE, t)
    row = lambda i: (i, 0)
    const = lambda i: (0, 0)
    return pl.pallas_call(
        _inproj_kernel,
        grid=(t // tm,),
        in_specs=[pl.BlockSpec((tm, d), row), pl.BlockSpec((1, d), const),
                  pl.BlockSpec((d, n), const)],
        out_specs=[pl.BlockSpec((tm, PA_W), row), pl.BlockSpec((tm, PB_W), row),
                   pl.BlockSpec((tm, PC_W), row), pl.BlockSpec((tm, PBA_W), row)],
        out_shape=[jax.ShapeDtypeStruct((t, PA_W), F32), jax.ShapeDtypeStruct((t, PB_W), F32),
                   jax.ShapeDtypeStruct((t, PC_W), F32), jax.ShapeDtypeStruct((t, PBA_W), F32)],
        compiler_params=pltpu.CompilerParams(dimension_semantics=("arbitrary",),
                                             vmem_limit_bytes=VMEM_LIMIT),
        name="inproj",
    )(xf, g, w)


def _outmlp_kernel(x_ref, oa_ref, ob_ref, oc_ref, wo_ref, g2_ref, wup_ref, wdn_ref, gf_ref,
                   o_ref, *, final):
    x1 = x_ref[...]
    x1 = x1 + _dot(oa_ref[...], wo_ref[0:A_WIDTH, :])
    x1 = x1 + _dot(ob_ref[...], wo_ref[A_WIDTH:A_WIDTH + B_WIDTH, :])
    x1 = x1 + _dot(oc_ref[...], wo_ref[A_WIDTH + B_WIDTH:, :])
    o_ref[...] = x1
    h2 = (_rms_scale(x1) * g2_ref[...]).astype(BF16)
    d_ff = wup_ref.shape[1]
    for c in range(d_ff // FF_CHUNK):
        u = jnp.maximum(_dot(h2, wup_ref[:, c * FF_CHUNK:(c + 1) * FF_CHUNK]), 0.0)
        o_ref[...] += _dot((u * u).astype(BF16), wdn_ref[c * FF_CHUNK:(c + 1) * FF_CHUNK, :])
    if final:
        o_ref[...] = _rms_scale(o_ref[...]) * gf_ref[...]


def _outmlp(xf, oa, ob, oc, wo, g2, wup, wdn, gf, final):
    t, d = xf.shape
    tm = min(PROJ_TILE, t)
    row = lambda i: (i, 0)
    const = lambda i: (0, 0)
    full = lambda a: pl.BlockSpec(a.shape, const)
    return pl.pallas_call(
        functools.partial(_outmlp_kernel, final=final),
        grid=(t // tm,),
        in_specs=[pl.BlockSpec((tm, d), row), pl.BlockSpec((tm, A_WIDTH), row),
                  pl.BlockSpec((tm, B_WIDTH), row), pl.BlockSpec((tm, C_WIDTH), row),
                  full(wo), full(g2), full(wup), full(wdn), full(gf)],
        out_specs=pl.BlockSpec((tm, d), row),
        out_shape=jax.ShapeDtypeStruct((t, d), F32),
        compiler_params=pltpu.CompilerParams(dimension_semantics=("arbitrary",),
                                             vmem_limit_bytes=VMEM_LIMIT),
        name="outmlp",
    )(xf, oa, ob, oc, wo, g2, wup, wdn, gf)


def _hgrn2_kernel(pa_ref, lbl_ref, g_ref, ones_ref, tri_ref, o_ref, st_ref, *, layer):
    nbat, tt = pa_ref.shape[0], pa_ref.shape[1]
    w = A_WIDTH
    rows = nbat * tt
    nsub = rows // SUB
    nsub_b = tt // SUB

    @pl.when(pl.program_id(1) == 0)
    def _():
        st_ref[...] = jnp.zeros_like(st_ref)

    logits = lbl_ref[...]
    ex = jnp.exp(logits - jnp.max(logits, axis=0, keepdims=True))
    p = ex / jnp.sum(ex, axis=0, keepdims=True)
    lb = jnp.clip(jnp.sum(p[0:layer + 1], axis=0, keepdims=True) - p[0:1], 0.0, 1.0 - EPS)

    def cols(j):
        return pa_ref[:, :, j * w:(j + 1) * w].reshape(rows, w)

    q = _silu(cols(0))
    fp = cols(1)
    v = cols(2)
    f = lb + (1.0 - lb) * _sigmoid(fp)
    logf = jnp.log(jnp.maximum(f, TINY))
    k = (1.0 - lb) * _sigmoid(-fp)

    ones_bd = ones_ref[...]
    tri = tri_ref[...]
    cum = jnp.concatenate([_mask_dot(tri, logf[b * tt:(b + 1) * tt]) for b in range(nbat)], axis=0)

    q3 = q.reshape(nsub, SUB, w)
    k3 = k.reshape(nsub, SUB, w)
    v3 = v.reshape(nsub, SUB, w)
    c3 = cum.reshape(nsub, SUB, w)
    row = lax.broadcasted_iota(jnp.int32, (nsub, SUB, w), 1)

    o_intra = jnp.zeros((nsub, SUB, w), F32)
    for j in range(SUB):
        keep = row >= j
        dec = jnp.where(keep, jnp.exp(c3 - c3[:, j:j + 1, :]), 0.0)
        x = (q3 * k3[:, j:j + 1, :] * dec).reshape(rows, w).astype(BF16)
        attn = _dot(x, ones_bd)
        o_intra = o_intra + attn.reshape(nsub, SUB, w) * v3[:, j:j + 1, :]

    c_last = c3[:, SUB - 1:SUB, :]
    qd = (q3 * jnp.exp(c3)).astype(BF16)
    kd = (k3 * jnp.exp(c_last - c3)).astype(BF16)
    v_bf = v3.astype(BF16)
    blk = (lax.broadcasted_iota(jnp.int32, (w, w), 0) // A_HEAD_DIM
           == lax.broadcasted_iota(jnp.int32, (w, w), 1) // A_HEAD_DIM)
    sts = [st_ref[b] for b in range(nbat)]
    o_parts = [[None] * nsub_b for _ in range(nbat)]
    for i in range(nsub_b):
        for b in range(nbat):
            n = b * nsub_b + i
            o_parts[b][i] = _dot_nt(qd[n], sts[b].astype(BF16)) + o_intra[n]
            upd = _dot_tn(v_bf[n], kd[n])
            sts[b] = sts[b] * jnp.exp(c_last[n]) + jnp.where(blk, upd, 0.0)
    for b in range(nbat):
        st_ref[b] = sts[b]
    o = jnp.concatenate([o_parts[b][i] for b in range(nbat) for i in range(nsub_b)], axis=0)

    ms = _dot_mask(o * o, ones_bd) * (1.0 / A_HEAD_DIM)
    o = o * lax.rsqrt(ms + EPS) * g_ref[...] * _silu(cols(3))
    o_ref[...] = o.reshape(nbat, tt, w).astype(o_ref.dtype)


def _hgrn2(pa, lb_logits, norm_g, layer, batch):
    t = pa.shape[0]
    s = t // batch
    tt = min(HGRN_TILE, s)
    nbat = min(MIX_BATCH, batch)
    w = A_WIDTH
    idx = np.arange(w)
    ones_bd = jnp.asarray(idx[:, None] // A_HEAD_DIM == idx[None, :] // A_HEAD_DIM, BF16)
    it = np.arange(tt)
    tri = jnp.asarray((it[:, None] // SUB == it[None, :] // SUB) & (it[:, None] >= it[None, :]), BF16)
    g = jnp.tile(norm_g.astype(F32), A_HEADS).reshape(1, w)
    tile = lambda b, i: (b, i, 0)
    const = lambda b, i: (0, 0)
    out = pl.pallas_call(
        functools.partial(_hgrn2_kernel, layer=layer),
        grid=(batch // nbat, s // tt),
        in_specs=[pl.BlockSpec((nbat, tt, PA_W), tile), pl.BlockSpec(lb_logits.shape, const),
                  pl.BlockSpec((1, w), const), pl.BlockSpec((w, w), const),
                  pl.BlockSpec((tt, tt), const)],
        out_specs=pl.BlockSpec((nbat, tt, w), tile),
        out_shape=jax.ShapeDtypeStruct((batch, s, w), BF16),
        scratch_shapes=[pltpu.VMEM((nbat, w, w), F32)],
        compiler_params=pltpu.CompilerParams(dimension_semantics=("arbitrary", "arbitrary"),
                                             vmem_limit_bytes=VMEM_LIMIT),
        name="hgrn2",
    )(pa.reshape(batch, s, PA_W), lb_logits.astype(F32), g, ones_bd, tri)
    return out.reshape(t, w)


def _gdn_kernel(pb_ref, pba_ref, cw_ref, alog_ref, dtb_ref, g_ref, tri_ref, o_ref,
                xbuf_ref, s_ref):
    nbat, tt = pb_ref.shape[0], pb_ref.shape[1]
    hd = B_HEAD_DIM
    rows = nbat * tt
    nchunk = tt // CHUNK
    nct = nbat * nchunk
    qkv_w = 3 * B_WIDTH

    @pl.when(pl.program_id(1) == 0)
    def _():
        s_ref[...] = jnp.zeros_like(s_ref)
        xbuf_ref[...] = jnp.zeros_like(xbuf_ref)

    qkv = _silu(jnp.concatenate(
        [_causal_conv(pb_ref[b, :, 0:qkv_w], xbuf_ref, b, cw_ref) for b in range(nbat)], axis=0))

    lane = lax.broadcasted_iota(jnp.int32, (1, LANES), 1)
    is_alpha = (lane >= B_HEADS) & (lane < 2 * B_HEADS)
    ba = pba_ref[...].reshape(rows, LANES)
    beta_all = _sigmoid(ba)
    rate = jnp.where(is_alpha, jnp.exp(alog_ref[...]), 0.0)
    log_alpha = -rate * _softplus(ba + dtb_ref[...])
    tri = tri_ref[...]
    gc_all = jnp.concatenate([_mask_dot(tri, log_alpha[b * tt:(b + 1) * tt]) for b in range(nbat)], axis=0)
    gc_t = gc_all.T

    r_i = lax.broadcasted_iota(jnp.int32, (CHUNK, CHUNK), 0)
    c_i = lax.broadcasted_iota(jnp.int32, (CHUNK, CHUNK), 1)
    causal = r_i >= c_i
    strict = r_i > c_i
    eye = (r_i == c_i).astype(F32)
    scale = hd ** -0.5

    nchain = B_HEADS * nbat

    def chunks(xs):
        x4 = [x.reshape(nbat, nchunk, CHUNK, x.shape[-1]) for x in xs]
        return jnp.concatenate([x4[h][:, c] for c in range(nchunk) for h in range(B_HEADS)], axis=0)

    q_l, k_l, v_l, beta_l, g_l, grow_l = [], [], [], [], [], []
    for h in range(B_HEADS):
        qh = qkv[:, h * hd:(h + 1) * hd]
        kh = qkv[:, B_WIDTH + h * hd:B_WIDTH + (h + 1) * hd]
        q_l.append(qh * lax.rsqrt(jnp.sum(qh * qh, axis=-1, keepdims=True) + EPS))
        k_l.append(kh * lax.rsqrt(jnp.sum(kh * kh, axis=-1, keepdims=True) + EPS))
        v_l.append(qkv[:, 2 * B_WIDTH + h * hd:2 * B_WIDTH + (h + 1) * hd])
        beta_l.append(jnp.broadcast_to(beta_all[:, h:h + 1], (rows, hd)))
        g_l.append(jnp.broadcast_to(gc_all[:, B_HEADS + h:B_HEADS + h + 1], (rows, hd)))
    for c in range(nchunk):
        for h in range(B_HEADS):
            for bb in range(nbat):
                r0 = (bb * nchunk + c) * CHUNK
                grow_l.append(gc_t[B_HEADS + h:B_HEADS + h + 1, r0:r0 + CHUNK][None])
    q3, k3, v3, beta3, g3 = chunks(q_l), chunks(k_l), chunks(v_l), chunks(beta_l), chunks(g_l)
    g_row3 = jnp.concatenate(grow_l, axis=0)
    eg3 = jnp.exp(g3)
    g_last3 = g3[:, CHUNK - 1:CHUNK, :]
    kb3 = k3 * beta3
    k_bf = k3.astype(BF16)
    decay = jnp.where(causal, jnp.exp(jnp.where(causal, g3[:, :, 0:CHUNK] - g_row3, 0.0)), 0.0)
    kq = _bdot_nt(jnp.concatenate([kb3, q3], axis=1).astype(BF16), k_bf)
    a_mat = jnp.where(strict, kq[:, 0:CHUNK] * decay, 0.0)
    qk = kq[:, CHUNK:2 * CHUNK] * scale * decay
    t_inv = eye - jnp.where(r_i // 2 == c_i // 2, a_mat, 0.0)
    b = 2
    while b < CHUNK:
        off = jnp.where((r_i // (2 * b) == c_i // (2 * b)) & (r_i // b != c_i // b), a_mat, 0.0)
        t_bf = t_inv.astype(BF16)
        t_inv = t_inv - _bdot(t_bf, _bdot(off.astype(BF16), t_bf).astype(BF16))
        b *= 2
    rhs = jnp.concatenate([v3 * beta3, kb3 * eg3], axis=2).astype(BF16)
    uw = _bdot(t_inv.astype(BF16), rhs)
    qd3 = q3 * (scale * eg3)
    kd3 = k3 * jnp.exp(g_last3 - g3)
    sdec3 = jnp.exp(g_last3)

    lhs1 = jnp.concatenate([uw[:, :, hd:2 * hd], qd3], axis=1).astype(BF16)
    lhs2 = jnp.concatenate([qk, jnp.swapaxes(kd3, 1, 2)], axis=1).astype(BF16)

    s = s_ref[...]
    o_c = []
    for c in range(nchunk):
        sl = slice(c * nchain, (c + 1) * nchain)
        m1 = _bdot(lhs1[sl], s.astype(BF16))
        v_new = uw[sl, :, 0:hd] - m1[:, 0:CHUNK]
        m2 = _bdot(lhs2[sl], v_new.astype(BF16))
        o_c.append(m1[:, CHUNK:2 * CHUNK] + m2[:, 0:CHUNK])
        s = s * sdec3[sl] + m2[:, CHUNK:CHUNK + hd]
    s_ref[...] = s
    for h in range(B_HEADS):
        for bb in range(nbat):
            o = jnp.concatenate([o_c[c][h * nbat + bb] for c in range(nchunk)], axis=0)
            z = pb_ref[bb, :, qkv_w + h * hd:qkv_w + (h + 1) * hd]
            o_ref[bb, :, h * hd:(h + 1) * hd] = (_rms_scale(o) * g_ref[...] * _silu(z)).astype(o_ref.dtype)


def _gdn(pb, pba, conv_w, a_log, dt_bias, norm_g, batch):
    t = pb.shape[0]
    s = t // batch
    tt = min(GDN_TILE, s)
    nbat = min(MIX_BATCH, batch)
    it = np.arange(tt)
    tri = jnp.asarray((it[:, None] // CHUNK == it[None, :] // CHUNK) & (it[:, None] >= it[None, :]), BF16)
    alog = jnp.zeros((1, LANES), F32).at[0, B_HEADS:2 * B_HEADS].set(a_log.astype(F32))
    dtb = jnp.zeros((1, LANES), F32).at[0, B_HEADS:2 * B_HEADS].set(dt_bias.astype(F32))
    g = norm_g.astype(F32).reshape(1, B_HEAD_DIM)
    tile = lambda b, i: (b, i, 0)
    const = lambda b, i: (0, 0)
    out = pl.pallas_call(
        _gdn_kernel,
        grid=(batch // nbat, s // tt),
        in_specs=[pl.BlockSpec((nbat, tt, PB_W), tile), pl.BlockSpec((nbat, tt, PBA_W), tile),
                  pl.BlockSpec(conv_w.shape, const), pl.BlockSpec((1, LANES), const),
                  pl.BlockSpec((1, LANES), const), pl.BlockSpec((1, B_HEAD_DIM), const),
                  pl.BlockSpec((tt, tt), const)],
        out_specs=pl.BlockSpec((nbat, tt, B_WIDTH), tile),
        out_shape=jax.ShapeDtypeStruct((batch, s, B_WIDTH), BF16),
        scratch_shapes=[pltpu.VMEM((nbat, SUBLANES, 3 * B_WIDTH), F32),
                        pltpu.VMEM((nbat * B_HEADS, B_HEAD_DIM, B_HEAD_DIM), F32)],
        compiler_params=pltpu.CompilerParams(dimension_semantics=("arbitrary", "arbitrary"),
                                             vmem_limit_bytes=VMEM_LIMIT),
        name="gdn",
    )(pb.reshape(batch, s, PB_W), pba.reshape(batch, s, PBA_W), conv_w.astype(F32), alog, dtb, g, tri)
    return out.reshape(t, B_WIDTH)


def _rglru_kernel(pc_ref, cw_ref, cb_ref, wa_ref, ba_ref, wx_ref, bx_ref, lam_ref, o_ref,
                  xbuf_ref, h_ref):
    tt = pc_ref.shape[0]
    w = C_WIDTH
    first = pl.program_id(1) == 0

    @pl.when(first)
    def _():
        h_ref[...] = jnp.zeros_like(h_ref)
        xbuf_ref[...] = jnp.zeros_like(xbuf_ref)

    xc = _causal_conv(pc_ref[:, 0:w], xbuf_ref, 0, cw_ref) + cb_ref[...]

    x_bf = xc.astype(BF16)
    r = _sigmoid(_dot(x_bf, wa_ref[...]) + ba_ref[...])
    gate_i = _sigmoid(_dot(x_bf, wx_ref[...]) + bx_ref[...])
    log_a = -RG_C * r * _softplus(-lam_ref[...])
    a = jnp.exp(log_a)
    row = lax.broadcasted_iota(jnp.int32, (tt, w), 0)
    mult = jnp.sqrt(jnp.maximum(-jnp.tanh(log_a) * (a * a + 1.0), EPS))
    mult = jnp.where(row + pl.program_id(1) * tt == 0, 1.0, mult)
    b = mult * gate_i * xc

    d = 1
    while d < tt:
        keep = row >= d
        a_prev = jnp.where(keep, pltpu.roll(a, d, axis=0), 1.0)
        b_prev = jnp.where(keep, pltpu.roll(b, d, axis=0), 0.0)
        b = a * b_prev + b
        a = a * a_prev
        d *= 2
    h = b + a * h_ref[0:1, :]
    h_ref[...] = jnp.broadcast_to(h[tt - 1:tt, :], h_ref.shape)

    y = pc_ref[:, w:2 * w]
    gelu = 0.5 * y * (1.0 + jnp.tanh(0.7978845608028654 * (y + 0.044715 * (y * y * y))))
    o_ref[...] = (gelu * h).astype(o_ref.dtype)


def _block_diag(wb):
    n, d, _ = wb.shape
    out = jnp.zeros((n * d, n * d), wb.dtype)
    for i in range(n):
        out = out.at[i * d:(i + 1) * d, i * d:(i + 1) * d].set(wb[i])
    return out


def _rglru(pc, conv_w, conv_b, w_a, b_a, w_x, b_x, lam, batch):
    t = pc.shape[0]
    s = t // batch
    tt = min(MIX_TILE, s)
    nt = s // tt
    w = C_WIDTH
    vec = lambda a: a.astype(F32).reshape(1, w)
    row = lambda b, i: (b * nt + i, 0)
    const = lambda b, i: (0, 0)
    vspec = pl.BlockSpec((1, w), const)
    mspec = pl.BlockSpec((w, w), const)
    return pl.pallas_call(
        _rglru_kernel,
        grid=(batch, nt),
        in_specs=[pl.BlockSpec((tt, PC_W), row), pl.BlockSpec(conv_w.shape, const), vspec,
                  mspec, vspec, mspec, vspec, vspec],
        out_specs=pl.BlockSpec((tt, w), row),
        out_shape=jax.ShapeDtypeStruct((t, w), BF16),
        scratch_shapes=[pltpu.VMEM((1, SUBLANES, w), F32), pltpu.VMEM((SUBLANES, w), F32)],
        compiler_params=pltpu.CompilerParams(dimension_semantics=("arbitrary", "arbitrary"),
                                             vmem_limit_bytes=VMEM_LIMIT),
        name="rglru",
    )(pc, conv_w.astype(F32), vec(conv_b), _block_diag(w_a).astype(BF16), vec(b_a),
      _block_diag(w_x).astype(BF16), vec(b_x), vec(lam))


def _reorder_w_in(w):
    a_end = PA_W
    b_end = a_end + PB_W
    ba_end = b_end + 2 * B_HEADS
    pad = jnp.zeros((w.shape[0], PBA_W - 2 * B_HEADS), w.dtype)
    return jnp.concatenate([w[:, :b_end], w[:, ba_end:], w[:, b_end:ba_end], pad], axis=1).astype(BF16)


def kernel(x, norm1_g, w_in, hgrn_lb_logits, hgrn_norm_g, gdn_conv_w, gdn_a_log, gdn_dt_bias, gdn_norm_g, lru_conv_w, lru_conv_b, lru_w_a, lru_b_a, lru_w_x, lru_b_x, lru_lambda, w_out, norm2_g, w_up, w_down, final_norm_g):
    bsz, seq, d = x.shape
    depth = w_in.shape[0]
    xf = x.reshape(bsz * seq, d).astype(F32)
    gf = final_norm_g.astype(F32).reshape(1, d)
    for l in range(depth):
        pa, pb, pc, pba = _inproj(xf, norm1_g[l].astype(F32).reshape(1, d), _reorder_w_in(w_in[l]))
        oa = _hgrn2(pa, hgrn_lb_logits, hgrn_norm_g[l], l, bsz)
        ob = _gdn(pb, pba, gdn_conv_w[l], gdn_a_log[l], gdn_dt_bias[l], gdn_norm_g[l], bsz)
        oc = _rglru(pc, lru_conv_w[l], lru_conv_b[l], lru_w_a[l], lru_b_a[l], lru_w_x[l], lru_b_x[l],
                    lru_lambda[l], bsz)
        xf = _outmlp(xf, oa, ob, oc, w_out[l].astype(BF16), norm2_g[l].astype(F32).reshape(1, d),
                     w_up[l].astype(BF16), w_down[l].astype(BF16), gf, l == depth - 1)
    return xf.reshape(bsz, seq, d).astype(x.dtype)
```

```python
import functools

import jax
import jax.numpy as jnp
import numpy as np
from jax import lax
from jax.experimental import pallas as pl
from jax.experimental.pallas import tpu as pltpu

F32 = jnp.float32
BF16 = jnp.bfloat16

A_HEADS = 4
A_HEAD_DIM = 64
A_WIDTH = A_HEADS * A_HEAD_DIM
B_HEADS = 4
B_HEAD_DIM = 128
B_WIDTH = B_HEADS * B_HEAD_DIM
C_BLOCKS = 4
C_WIDTH = 256
C_BLOCK_DIM = C_WIDTH // C_BLOCKS
CONV_K = 4
CHUNK = 64
SUB = 16
RG_C = 8.0
EPS = 1e-6
TINY = 1e-30

LANES = 128
SUBLANES = 8
VMEM_LIMIT = 56 * 1024 * 1024

PROJ_TILE = 512
MIX_TILE = 256
MIX_BATCH = 4
HGRN_TILE = 128
GDN_TILE = 128
FF_CHUNK = 1024

PA_W = 4 * A_WIDTH
PB_W = 4 * B_WIDTH
PC_W = 2 * C_WIDTH
PBA_W = LANES


def _sigmoid(x):
    return 1.0 / (1.0 + jnp.exp(-x))


def _silu(x):
    h = 0.5 * x
    return h * (1.0 + jnp.tanh(h))


def _softplus(x):
    return jnp.maximum(x, 0.0) + jnp.log1p(jnp.exp(-jnp.abs(x)))


def _split3(x):
    hi = x.astype(BF16)
    r1 = x - hi.astype(F32)
    mid = r1.astype(BF16)
    lo = (r1 - mid.astype(F32)).astype(BF16)
    return hi, mid, lo


def _dot(a, b):
    return jnp.dot(a, b, preferred_element_type=F32)


def _dot_nt(a, b):
    return lax.dot_general(a, b, (((1,), (1,)), ((), ())), preferred_element_type=F32)


def _dot_tn(a, b):
    return lax.dot_general(a, b, (((0,), (0,)), ((), ())), preferred_element_type=F32)


def _bdot(a, b):
    return lax.dot_general(a, b, (((2,), (1,)), ((0,), (0,))), preferred_element_type=F32)


def _bdot_nt(a, b):
    return lax.dot_general(a, b, (((2,), (2,)), ((0,), (0,))), preferred_element_type=F32)


def _mask_dot(m01, x):
    hi, mid, lo = _split3(x)
    return _dot(m01, hi) + _dot(m01, mid) + _dot(m01, lo)


def _dot_mask(x, m01):
    hi, mid, lo = _split3(x)
    return _dot(hi, m01) + _dot(mid, m01) + _dot(lo, m01)


def _causal_conv(x, carry_ref, b, w_ref):
    tt = x.shape[0]
    pad = SUBLANES
    ext = jnp.concatenate([carry_ref[b], x], axis=0)
    y = x * w_ref[CONV_K - 1:CONV_K, :]
    for s in range(1, CONV_K):
        y = y + pltpu.roll(ext, s, axis=0)[pad:pad + tt] * w_ref[CONV_K - 1 - s:CONV_K - s, :]
    carry_ref[b] = x[tt - pad:tt]
    return y


def _rms_scale(x):
    return x * lax.rsqrt(jnp.mean(x * x, axis=-1, keepdims=True) + EPS)


def _inproj_kernel(x_ref, g_ref, w_ref, pa_ref, pb_ref, pc_ref, pba_ref):
    h = (_rms_scale(x_ref[...]) * g_ref[...]).astype(BF16)
    off = 0
    for ref, width in ((pa_ref, PA_W), (pb_ref, PB_W), (pc_ref, PC_W), (pba_ref, PBA_W)):
        ref[...] = _dot(h, w_ref[:, off:off + width])
        off += width


def _inproj(xf, g, w):
    t, d = xf.shape
    n = w.shape[1]
    tm = min(PROJ_TILE, t)
    row = lambda i: (i, 0)
    const = lambda i: (0, 0)
    return pl.pallas_call(
        _inproj_kernel,
        grid=(t // tm,),
        in_specs=[pl.BlockSpec((tm, d), row), pl.BlockSpec((1, d), const),
                  pl.BlockSpec((d, n), const)],
        out_specs=[pl.BlockSpec((tm, PA_W), row), pl.BlockSpec((tm, PB_W), row),
                   pl.BlockSpec((tm, PC_W), row), pl.BlockSpec((tm, PBA_W), row)],
        out_shape=[jax.ShapeDtypeStruct((t, PA_W), F32), jax.ShapeDtypeStruct((t, PB_W), F32),
                   jax.ShapeDtypeStruct((t, PC_W), F32), jax.ShapeDtypeStruct((t, PBA_W), F32)],
        compiler_params=pltpu.CompilerParams(dimension_semantics=("arbitrary",),
                                             vmem_limit_bytes=VMEM_LIMIT),
        name="inproj",
    )(xf, g, w)


def _outmlp_kernel(x_ref, oa_ref, ob_ref, oc_ref, wo_ref, g2_ref, wup_ref, wdn_ref, gf_ref,
                   o_ref, *, final):
    x1 = x_ref[...]
    x1 = x1 + _dot(oa_ref[...], wo_ref[0:A_WIDTH, :])
    x1 = x1 + _dot(ob_ref[...], wo_ref[A_WIDTH:A_WIDTH + B_WIDTH, :])
    x1 = x1 + _dot(oc_ref[...], wo_ref[A_WIDTH + B_WIDTH:, :])
    o_ref[...] = x1
    h2 = (_rms_scale(x1) * g2_ref[...]).astype(BF16)
    d_ff = wup_ref.shape[1]
    for c in range(d_ff // FF_CHUNK):
        u = jnp.maximum(_dot(h2, wup_ref[:, c * FF_CHUNK:(c + 1) * FF_CHUNK]), 0.0)
        o_ref[...] += _dot((u * u).astype(BF16), wdn_ref[c * FF_CHUNK:(c + 1) * FF_CHUNK, :])
    if final:
        o_ref[...] = _rms_scale(o_ref[...]) * gf_ref[...]


def _outmlp(xf, oa, ob, oc, wo, g2, wup, wdn, gf, final):
    t, d = xf.shape
    tm = min(PROJ_TILE, t)
    row = lambda i: (i, 0)
    const = lambda i: (0, 0)
    full = lambda a: pl.BlockSpec(a.shape, const)
    return pl.pallas_call(
        functools.partial(_outmlp_kernel, final=final),
        grid=(t // tm,),
        in_specs=[pl.BlockSpec((tm, d), row), pl.BlockSpec((tm, A_WIDTH), row),
                  pl.BlockSpec((tm, B_WIDTH), row), pl.BlockSpec((tm, C_WIDTH), row),
                  full(wo), full(g2), full(wup), full(wdn), full(gf)],
        out_specs=pl.BlockSpec((tm, d), row),
        out_shape=jax.ShapeDtypeStruct((t, d), F32),
        compiler_params=pltpu.CompilerParams(dimension_semantics=("arbitrary",),
                                             vmem_limit_bytes=VMEM_LIMIT),
        name="outmlp",
    )(xf, oa, ob, oc, wo, g2, wup, wdn, gf)


def _hgrn2_kernel(pa_ref, lbl_ref, g_ref, ones_ref, tri_ref, o_ref, st_ref, *, layer):
    nbat, tt = pa_ref.shape[0], pa_ref.shape[1]
    w = A_WIDTH
    rows = nbat * tt
    nsub = rows // SUB
    nsub_b = tt // SUB

    @pl.when(pl.program_id(1) == 0)
    def _():
        st_ref[...] = jnp.zeros_like(st_ref)

    logits = lbl_ref[...]
    ex = jnp.exp(logits - jnp.max(logits, axis=0, keepdims=True))
    p = ex / jnp.sum(ex, axis=0, keepdims=True)
    lb = jnp.clip(jnp.sum(p[0:layer + 1], axis=0, keepdims=True) - p[0:1], 0.0, 1.0 - EPS)

    def cols(j):
        return pa_ref[:, :, j * w:(j + 1) * w].reshape(rows, w)

    q = _silu(cols(0))
    fp = cols(1)
    v = cols(2)
    f = lb + (1.0 - lb) * _sigmoid(fp)
    logf = jnp.log(jnp.maximum(f, TINY))
    k = (1.0 - lb) * _sigmoid(-fp)

    ones_bd = ones_ref[...]
    tri = tri_ref[...]
    cum = jnp.concatenate([_mask_dot(tri, logf[b * tt:(b + 1) * tt]) for b in range(nbat)], axis=0)

    q3 = q.reshape(nsub, SUB, w)
    k3 = k.reshape(nsub, SUB, w)
    v3 = v.reshape(nsub, SUB, w)
    c3 = cum.reshape(nsub, SUB, w)
    lk3 = (cum - jnp.log(k)).reshape(nsub, SUB, w)
    row = lax.broadcasted_iota(jnp.int32, (nsub, SUB, w), 1)

    o_intra = jnp.zeros((nsub, SUB, w), F32)
    for j in range(SUB):
        keep = row >= j
        dec = jnp.where(keep, jnp.exp(c3 - lk3[:, j:j + 1, :]), 0.0)
        x = (q3 * dec).reshape(rows, w).astype(BF16)
        attn = _dot(x, ones_bd)
        o_intra = o_intra + attn.reshape(nsub, SUB, w) * v3[:, j:j + 1, :]

    c_last = c3[:, SUB - 1:SUB, :]
    qd = (q3 * jnp.exp(c3)).astype(BF16)
    kd = (k3 * jnp.exp(c_last - c3)).astype(BF16)
    v_bf = v3.astype(BF16)
    blk = (lax.broadcasted_iota(jnp.int32, (w, w), 0) // A_HEAD_DIM
           == lax.broadcasted_iota(jnp.int32, (w, w), 1) // A_HEAD_DIM)
    sts = [st_ref[b] for b in range(nbat)]
    o_parts = [[None] * nsub_b for _ in range(nbat)]
    for i in range(nsub_b):
        for b in range(nbat):
            n = b * nsub_b + i
            o_parts[b][i] = _dot_nt(qd[n], sts[b].astype(BF16)) + o_intra[n]
            upd = _dot_tn(v_bf[n], kd[n])
            sts[b] = sts[b] * jnp.exp(c_last[n]) + jnp.where(blk, upd, 0.0)
    for b in range(nbat):
        st_ref[b] = sts[b]
    o = jnp.concatenate([o_parts[b][i] for b in range(nbat) for i in range(nsub_b)], axis=0)

    ms = _dot_mask(o * o, ones_bd) * (1.0 / A_HEAD_DIM)
    o = o * lax.rsqrt(ms + EPS) * g_ref[...] * _silu(cols(3))
    o_ref[...] = o.reshape(nbat, tt, w).astype(o_ref.dtype)


def _hgrn2(pa, lb_logits, norm_g, layer, batch):
    t = pa.shape[0]
    s = t // batch
    tt = min(HGRN_TILE, s)
    nbat = min(MIX_BATCH, batch)
    w = A_WIDTH
    idx = np.arange(w)
    ones_bd = jnp.asarray(idx[:, None] // A_HEAD_DIM == idx[None, :] // A_HEAD_DIM, BF16)
    it = np.arange(tt)
    tri = jnp.asarray((it[:, None] // SUB == it[None, :] // SUB) & (it[:, None] >= it[None, :]), BF16)
    g = jnp.tile(norm_g.astype(F32), A_HEADS).reshape(1, w)
    tile = lambda b, i: (b, i, 0)
    const = lambda b, i: (0, 0)
    out = pl.pallas_call(
        functools.partial(_hgrn2_kernel, layer=layer),
        grid=(batch // nbat, s // tt),
        in_specs=[pl.BlockSpec((nbat, tt, PA_W), tile), pl.BlockSpec(lb_logits.shape, const),
                  pl.BlockSpec((1, w), const), pl.BlockSpec((w, w), const),
                  pl.BlockSpec((tt, tt), const)],
        out_specs=pl.BlockSpec((nbat, tt, w), tile),
        out_shape=jax.ShapeDtypeStruct((batch, s, w), BF16),
        scratch_shapes=[pltpu.VMEM((nbat, w, w), F32)],
        compiler_params=pltpu.CompilerParams(dimension_semantics=("arbitrary", "arbitrary"),
                                             vmem_limit_bytes=VMEM_LIMIT),
        name="hgrn2",
    )(pa.reshape(batch, s, PA_W), lb_logits.astype(F32), g, ones_bd, tri)
    return out.reshape(t, w)


def _gdn_kernel(pb_ref, pba_ref, cw_ref, alog_ref, dtb_ref, g_ref, tri_ref, o_ref,
                xbuf_ref, s_ref):
    nbat, tt = pb_ref.shape[0], pb_ref.shape[1]
    hd = B_HEAD_DIM
    rows = nbat * tt
    nchunk = tt // CHUNK
    nct = nbat * nchunk
    qkv_w = 3 * B_WIDTH

    @pl.when(pl.program_id(1) == 0)
    def _():
        s_ref[...] = jnp.zeros_like(s_ref)
        xbuf_ref[...] = jnp.zeros_like(xbuf_ref)

    qkv = _silu(jnp.concatenate(
        [_causal_conv(pb_ref[b, :, 0:qkv_w], xbuf_ref, b, cw_ref) for b in range(nbat)], axis=0))

    lane = lax.broadcasted_iota(jnp.int32, (1, LANES), 1)
    is_alpha = (lane >= B_HEADS) & (lane < 2 * B_HEADS)
    ba = pba_ref[...].reshape(rows, LANES)
    beta_all = _sigmoid(ba)
    rate = jnp.where(is_alpha, jnp.exp(alog_ref[...]), 0.0)
    log_alpha = -rate * _softplus(ba + dtb_ref[...])
    tri = tri_ref[...]
    gc_all = jnp.concatenate([_mask_dot(tri, log_alpha[b * tt:(b + 1) * tt]) for b in range(nbat)], axis=0)
    gc_t = gc_all.T

    r_i = lax.broadcasted_iota(jnp.int32, (CHUNK, CHUNK), 0)
    c_i = lax.broadcasted_iota(jnp.int32, (CHUNK, CHUNK), 1)
    causal = r_i >= c_i
    strict = r_i > c_i
    eye = (r_i == c_i).astype(F32)
    scale = hd ** -0.5

    nchain = B_HEADS * nbat

    def chunks(xs):
        x4 = [x.reshape(nbat, nchunk, CHUNK, x.shape[-1]) for x in xs]
        return jnp.concatenate([x4[h][:, c] for c in range(nchunk) for h in range(B_HEADS)], axis=0)

    q_l, k_l, v_l, beta_l, g_l, grow_l = [], [], [], [], [], []
    for h in range(B_HEADS):
        qh = qkv[:, h * hd:(h + 1) * hd]
        kh = qkv[:, B_WIDTH + h * hd:B_WIDTH + (h + 1) * hd]
        q_l.append(qh * lax.rsqrt(jnp.sum(qh * qh, axis=-1, keepdims=True) + EPS))
        k_l.append(kh * lax.rsqrt(jnp.sum(kh * kh, axis=-1, keepdims=True) + EPS))
        v_l.append(qkv[:, 2 * B_WIDTH + h * hd:2 * B_WIDTH + (h + 1) * hd])
        beta_l.append(jnp.broadcast_to(beta_all[:, h:h + 1], (rows, hd)))
        g_l.append(jnp.broadcast_to(gc_all[:, B_HEADS + h:B_HEADS + h + 1], (rows, hd)))
    for c in range(nchunk):
        for h in range(B_HEADS):
            for bb in range(nbat):
                r0 = (bb * nchunk + c) * CHUNK
                grow_l.append(gc_t[B_HEADS + h:B_HEADS + h + 1, r0:r0 + CHUNK][None])
    q3, k3, v3, beta3, g3 = chunks(q_l), chunks(k_l), chunks(v_l), chunks(beta_l), chunks(g_l)
    g_row3 = jnp.concatenate(grow_l, axis=0)
    eg3 = jnp.exp(g3)
    g_last3 = g3[:, CHUNK - 1:CHUNK, :]
    kb3 = k3 * beta3
    k_bf = k3.astype(BF16)
    decay = jnp.where(causal, jnp.exp(jnp.where(causal, g3[:, :, 0:CHUNK] - g_row3, 0.0)), 0.0)
    kq = _bdot_nt(jnp.concatenate([kb3, q3], axis=1).astype(BF16), k_bf)
    a_mat = jnp.where(strict, kq[:, 0:CHUNK] * decay, 0.0)
    qk = kq[:, CHUNK:2 * CHUNK] * scale * decay
    t_inv = eye - jnp.where(r_i // 2 == c_i // 2, a_mat, 0.0)
    b = 2
    while b < CHUNK:
        off = jnp.where((r_i // (2 * b) == c_i // (2 * b)) & (r_i // b != c_i // b), a_mat, 0.0)
        t_bf = t_inv.astype(BF16)
        t_inv = t_inv - _bdot(t_bf, _bdot(off.astype(BF16), t_bf).astype(BF16))
        b *= 2
    rhs = jnp.concatenate([v3 * beta3, kb3 * eg3], axis=2).astype(BF16)
    uw = _bdot(t_inv.astype(BF16), rhs)
    qd3 = q3 * (scale * eg3)
    kd3 = k3 * jnp.exp(g_last3 - g3)
    sdec3 = jnp.exp(g_last3)

    lhs1 = jnp.concatenate([uw[:, :, hd:2 * hd], qd3], axis=1).astype(BF16)
    lhs2 = jnp.concatenate([qk, jnp.swapaxes(kd3, 1, 2)], axis=1).astype(BF16)

    s = s_ref[...]
    o_c = []
    for c in range(nchunk):
        sl = slice(c * nchain, (c + 1) * nchain)
        m1 = _bdot(lhs1[sl], s.astype(BF16))
        v_new = uw[sl, :, 0:hd] - m1[:, 0:CHUNK]
        m2 = _bdot(lhs2[sl], v_new.astype(BF16))
        o_c.append(m1[:, CHUNK:2 * CHUNK] + m2[:, 0:CHUNK])
        s = s * sdec3[sl] + m2[:, CHUNK:CHUNK + hd]
    s_ref[...] = s
    for h in range(B_HEADS):
        for bb in range(nbat):
            o = jnp.concatenate([o_c[c][h * nbat + bb] for c in range(nchunk)], axis=0)
            z = pb_ref[bb, :, qkv_w + h * hd:qkv_w + (h + 1) * hd]
            o_ref[bb, :, h * hd:(h + 1) * hd] = (_rms_scale(o) * g_ref[...] * _silu(z)).astype(o_ref.dtype)


def _gdn(pb, pba, conv_w, a_log, dt_bias, norm_g, batch):
    t = pb.shape[0]
    s = t // batch
    tt = min(GDN_TILE, s)
    nbat = min(MIX_BATCH, batch)
    it = np.arange(tt)
    tri = jnp.asarray((it[:, None] // CHUNK == it[None, :] // CHUNK) & (it[:, None] >= it[None, :]), BF16)
    alog = jnp.zeros((1, LANES), F32).at[0, B_HEADS:2 * B_HEADS].set(a_log.astype(F32))
    dtb = jnp.zeros((1, LANES), F32).at[0, B_HEADS:2 * B_HEADS].set(dt_bias.astype(F32))
    g = norm_g.astype(F32).reshape(1, B_HEAD_DIM)
    tile = lambda b, i: (b, i, 0)
    const = lambda b, i: (0, 0)
    out = pl.pallas_call(
        _gdn_kernel,
        grid=(batch // nbat, s // tt),
        in_specs=[pl.BlockSpec((nbat, tt, PB_W), tile), pl.BlockSpec((nbat, tt, PBA_W), tile),
                  pl.BlockSpec(conv_w.shape, const), pl.BlockSpec((1, LANES), const),
                  pl.BlockSpec((1, LANES), const), pl.BlockSpec((1, B_HEAD_DIM), const),
                  pl.BlockSpec((tt, tt), const)],
        out_specs=pl.BlockSpec((nbat, tt, B_WIDTH), tile),
        out_shape=jax.ShapeDtypeStruct((batch, s, B_WIDTH), BF16),
        scratch_shapes=[pltpu.VMEM((nbat, SUBLANES, 3 * B_WIDTH), F32),
                        pltpu.VMEM((nbat * B_HEADS, B_HEAD_DIM, B_HEAD_DIM), F32)],
        compiler_params=pltpu.CompilerParams(dimension_semantics=("arbitrary", "arbitrary"),
                                             vmem_limit_bytes=VMEM_LIMIT),
        name="gdn",
    )(pb.reshape(batch, s, PB_W), pba.reshape(batch, s, PBA_W), conv_w.astype(F32), alog, dtb, g, tri)
    return out.reshape(t, B_WIDTH)


def _rglru_kernel(pc_ref, cw_ref, cb_ref, wa_ref, ba_ref, wx_ref, bx_ref, lam_ref, o_ref,
                  xbuf_ref, h_ref):
    tt = pc_ref.shape[0]
    w = C_WIDTH
    first = pl.program_id(1) == 0

    @pl.when(first)
    def _():
        h_ref[...] = jnp.zeros_like(h_ref)
        xbuf_ref[...] = jnp.zeros_like(xbuf_ref)

    xc = _causal_conv(pc_ref[:, 0:w], xbuf_ref, 0, cw_ref) + cb_ref[...]

    x_bf = xc.astype(BF16)
    r = _sigmoid(_dot(x_bf, wa_ref[...]) + ba_ref[...])
    gate_i = _sigmoid(_dot(x_bf, wx_ref[...]) + bx_ref[...])
    log_a = -RG_C * r * _softplus(-lam_ref[...])
    a = jnp.exp(log_a)
    row = lax.broadcasted_iota(jnp.int32, (tt, w), 0)
    mult = jnp.sqrt(jnp.maximum(-jnp.tanh(log_a) * (a * a + 1.0), EPS))
    mult = jnp.where(row + pl.program_id(1) * tt == 0, 1.0, mult)
    b = mult * gate_i * xc

    d = 1
    while d < tt:
        keep = row >= d
        a_prev = jnp.where(keep, pltpu.roll(a, d, axis=0), 1.0)
        b_prev = jnp.where(keep, pltpu.roll(b, d, axis=0), 0.0)
        b = a * b_prev + b
        a = a * a_prev
        d *= 2
    h = b + a * h_ref[0:1, :]
    h_ref[...] = jnp.broadcast_to(h[tt - 1:tt, :], h_ref.shape)

    y = pc_ref[:, w:2 * w]
    gelu = 0.5 * y * (1.0 + jnp.tanh(0.7978845608028654 * (y + 0.044715 * (y * y * y))))
    o_ref[...] = (gelu * h).astype(o_ref.dtype)


def _block_diag(wb):
    n, d, _ = wb.shape
    out = jnp.zeros((n * d, n * d), wb.dtype)
    for i in range(n):
        out = out.at[i * d:(i + 1) * d, i * d:(i + 1) * d].set(wb[i])
    return out


def _rglru(pc, conv_w, conv_b, w_a, b_a, w_x, b_x, lam, batch):
    t = pc.shape[0]
    s = t // batch
    tt = min(MIX_TILE, s)
    nt = s // tt
    w = C_WIDTH
    vec = lambda a: a.astype(F32).reshape(1, w)
    row = lambda b, i: (b * nt + i, 0)
    const = lambda b, i: (0, 0)
    vspec = pl.BlockSpec((1, w), const)
    mspec = pl.BlockSpec((w, w), const)
    return pl.pallas_call(
        _rglru_kernel,
        grid=(batch, nt),
        in_specs=[pl.BlockSpec((tt, PC_W), row), pl.BlockSpec(conv_w.shape, const), vspec,
                  mspec, vspec, mspec, vspec, vspec],
        out_specs=pl.BlockSpec((tt, w), row),
        out_shape=jax.ShapeDtypeStruct((t, w), BF16),
        scratch_shapes=[pltpu.VMEM((1, SUBLANES, w), F32), pltpu.VMEM((SUBLANES, w), F32)],
        compiler_params=pltpu.CompilerParams(dimension_semantics=("arbitrary", "arbitrary"),
                                             vmem_limit_bytes=VMEM_LIMIT),
        name="rglru",
    )(pc, conv_w.astype(F32), vec(conv_b), _block_diag(w_a).astype(BF16), vec(b_a),
      _block_diag(w_x).astype(BF16), vec(b_x), vec(lam))


def _reorder_w_in(w):
    a_end = PA_W
    b_end = a_end + PB_W
    ba_end = b_end + 2 * B_HEADS
    pad = jnp.zeros((w.shape[0], PBA_W - 2 * B_HEADS), w.dtype)
    return jnp.concatenate([w[:, :b_end], w[:, ba_end:], w[:, b_end:ba_end], pad], axis=1).astype(BF16)


def kernel(x, norm1_g, w_in, hgrn_lb_logits, hgrn_norm_g, gdn_conv_w, gdn_a_log, gdn_dt_bias, gdn_norm_g, lru_conv_w, lru_conv_b, lru_w_a, lru_b_a, lru_w_x, lru_b_x, lru_lambda, w_out, norm2_g, w_up, w_down, final_norm_g):
    bsz, seq, d = x.shape
    depth = w_in.shape[0]
    xf = x.reshape(bsz * seq, d).astype(F32)
    gf = final_norm_g.astype(F32).reshape(1, d)
    for l in range(depth):
        pa, pb, pc, pba = _inproj(xf, norm1_g[l].astype(F32).reshape(1, d), _reorder_w_in(w_in[l]))
        oa = _hgrn2(pa, hgrn_lb_logits, hgrn_norm_g[l], l, bsz)
        ob = _gdn(pb, pba, gdn_conv_w[l], gdn_a_log[l], gdn_dt_bias[l], gdn_norm_g[l], bsz)
        oc = _rglru(pc, lru_conv_w[l], lru_conv_b[l], lru_w_a[l], lru_b_a[l], lru_w_x[l], lru_b_x[l],
                    lru_lambda[l], bsz)
        xf = _outmlp(xf, oa, ob, oc, w_out[l].astype(BF16), norm2_g[l].astype(F32).reshape(1, d),
                     w_up[l].astype(BF16), w_down[l].astype(BF16), gf, l == depth - 1)
    return xf.reshape(bsz, seq, d).astype(x.dtype)
```

```python
import functools

import jax
import jax.numpy as jnp
import numpy as np
from jax import lax
from jax.experimental import pallas as pl
from jax.experimental.pallas import tpu as pltpu

F32 = jnp.float32
BF16 = jnp.bfloat16

A_HEADS = 4
A_HEAD_DIM = 64
A_WIDTH = A_HEADS * A_HEAD_DIM
B_HEADS = 4
B_HEAD_DIM = 128
B_WIDTH = B_HEADS * B_HEAD_DIM
C_BLOCKS = 4
C_WIDTH = 256
C_BLOCK_DIM = C_WIDTH // C_BLOCKS
CONV_K = 4
CHUNK = 64
SUB = 16
RG_C = 8.0
EPS = 1e-6
TINY = 1e-30

LANES = 128
SUBLANES = 8
VMEM_LIMIT = 56 * 1024 * 1024

PROJ_TILE = 512
MIX_TILE = 256
MIX_BATCH = 4
HGRN_TILE = 128
GDN_TILE = 128
FF_CHUNK = 1024

PA_W = 4 * A_WIDTH
PB_W = 4 * B_WIDTH
PC_W = 2 * C_WIDTH
PBA_W = LANES


def _sigmoid(x):
    return 1.0 / (1.0 + jnp.exp(-x))


def _silu(x):
    h = 0.5 * x
    return h * (1.0 + jnp.tanh(h))


def _softplus(x):
    return jnp.maximum(x, 0.0) + jnp.log1p(jnp.exp(-jnp.abs(x)))


def _split3(x):
    hi = x.astype(BF16)
    r1 = x - hi.astype(F32)
    mid = r1.astype(BF16)
    lo = (r1 - mid.astype(F32)).astype(BF16)
    return hi, mid, lo


def _dot(a, b):
    return jnp.dot(a, b, preferred_element_type=F32)


def _dot_nt(a, b):
    return lax.dot_general(a, b, (((1,), (1,)), ((), ())), preferred_element_type=F32)


def _dot_tn(a, b):
    return lax.dot_general(a, b, (((0,), (0,)), ((), ())), preferred_element_type=F32)


def _bdot(a, b):
    return lax.dot_general(a, b, (((2,), (1,)), ((0,), (0,))), preferred_element_type=F32)


def _bdot_nt(a, b):
    return lax.dot_general(a, b, (((2,), (2,)), ((0,), (0,))), preferred_element_type=F32)


def _mask_dot(m01, x):
    hi, mid, lo = _split3(x)
    return _dot(m01, hi) + _dot(m01, mid) + _dot(m01, lo)


def _dot_mask(x, m01):
    hi, mid, lo = _split3(x)
    return _dot(hi, m01) + _dot(mid, m01) + _dot(lo, m01)


def _causal_conv(x, carry_ref, b, w_ref):
    tt = x.shape[0]
    pad = SUBLANES
    ext = jnp.concatenate([carry_ref[b], x], axis=0)
    y = x * w_ref[CONV_K - 1:CONV_K, :]
    for s in range(1, CONV_K):
        y = y + pltpu.roll(ext, s, axis=0)[pad:pad + tt] * w_ref[CONV_K - 1 - s:CONV_K - s, :]
    carry_ref[b] = x[tt - pad:tt]
    return y


def _rms_scale(x):
    return x * lax.rsqrt(jnp.mean(x * x, axis=-1, keepdims=True) + EPS)


def _inproj_kernel(x_ref, g_ref, w_ref, pa_ref, pb_ref, pc_ref, pba_ref):
    h = (_rms_scale(x_ref[...]) * g_ref[...]).astype(BF16)
    off = 0
    for ref, width in ((pa_ref, PA_W), (pb_ref, PB_W), (pc_ref, PC_W), (pba_ref, PBA_W)):
        ref[...] = _dot(h, w_ref[:, off:off + width])
        off += width


def _inproj(xf, g, w):
    t, d = xf.shape
    n = w.shape[1]
    tm = min(PROJ_TILE, t)
    row = lambda i: (i, 0)
    const = lambda i: (0, 0)
    return pl.pallas_call(
        _inproj_kernel,
        grid=(t // tm,),
        in_specs=[pl.BlockSpec((tm, d), row), pl.BlockSpec((1, d), const),
                  pl.BlockSpec((d, n), const)],
        out_specs=[pl.BlockSpec((tm, PA_W), row), pl.BlockSpec((tm, PB_W), row),
                   pl.BlockSpec((tm, PC_W), row), pl.BlockSpec((tm, PBA_W), row)],
        out_shape=[jax.ShapeDtypeStruct((t, PA_W), F32), jax.ShapeDtypeStruct((t, PB_W), F32),
                   jax.ShapeDtypeStruct((t, PC_W), F32), jax.ShapeDtypeStruct((t, PBA_W), F32)],
        compiler_params=pltpu.CompilerParams(dimension_semantics=("arbitrary",),
                                             vmem_limit_bytes=VMEM_LIMIT),
        name="inproj",
    )(xf, g, w)


def _outmlp_kernel(x_ref, oa_ref, ob_ref, oc_ref, wo_ref, g2_ref, wup_ref, wdn_ref, gf_ref,
                   o_ref, *, final):
    x1 = x_ref[...]
    x1 = x1 + _dot(oa_ref[...], wo_ref[0:A_WIDTH, :])
    x1 = x1 + _dot(ob_ref[...], wo_ref[A_WIDTH:A_WIDTH + B_WIDTH, :])
    x1 = x1 + _dot(oc_ref[...], wo_ref[A_WIDTH + B_WIDTH:, :])
    o_ref[...] = x1
    h2 = (_rms_scale(x1) * g2_ref[...]).astype(BF16)
    d_ff = wup_ref.shape[1]
    for c in range(d_ff // FF_CHUNK):
        u = jnp.maximum(_dot(h2, wup_ref[:, c * FF_CHUNK:(c + 1) * FF_CHUNK]), 0.0)
        o_ref[...] += _dot((u * u).astype(BF16), wdn_ref[c * FF_CHUNK:(c + 1) * FF_CHUNK, :])
    if final:
        o_ref[...] = _rms_scale(o_ref[...]) * gf_ref[...]


def _outmlp(xf, oa, ob, oc, wo, g2, wup, wdn, gf, final):
    t, d = xf.shape
    tm = min(PROJ_TILE, t)
    row = lambda i: (i, 0)
    const = lambda i: (0, 0)
    full = lambda a: pl.BlockSpec(a.shape, const)
    return pl.pallas_call(
        functools.partial(_outmlp_kernel, final=final),
        grid=(t // tm,),
        in_specs=[pl.BlockSpec((tm, d), row), pl.BlockSpec((tm, A_WIDTH), row),
                  pl.BlockSpec((tm, B_WIDTH), row), pl.BlockSpec((tm, C_WIDTH), row),
                  full(wo), full(g2), full(wup), full(wdn), full(gf)],
        out_specs=pl.BlockSpec((tm, d), row),
        out_shape=jax.ShapeDtypeStruct((t, d), F32),
        compiler_params=pltpu.CompilerParams(dimension_semantics=("arbitrary",),
                                             vmem_limit_bytes=VMEM_LIMIT),
        name="outmlp",
    )(xf, oa, ob, oc, wo, g2, wup, wdn, gf)


def _hgrn2_kernel(pa_ref, lbl_ref, g_ref, ones_ref, tri_ref, o_ref, st_ref, *, layer):
    nbat, tt = pa_ref.shape[0], pa_ref.shape[1]
    w = A_WIDTH
    rows = nbat * tt
    nsub = rows // SUB
    nsub_b = tt // SUB

    @pl.when(pl.program_id(1) == 0)
    def _():
        st_ref[...] = jnp.zeros_like(st_ref)

    logits = lbl_ref[...]
    ex = jnp.exp(logits - jnp.max(logits, axis=0, keepdims=True))
    p = ex / jnp.sum(ex, axis=0, keepdims=True)
    lb = jnp.clip(jnp.sum(p[0:layer + 1], axis=0, keepdims=True) - p[0:1], 0.0, 1.0 - EPS)

    def cols(j):
        return pa_ref[:, :, j * w:(j + 1) * w].reshape(rows, w)

    q = _silu(cols(0))
    fp = cols(1)
    v = cols(2)
    f = lb + (1.0 - lb) * _sigmoid(fp)
    logf = jnp.log(jnp.maximum(f, TINY))
    k = (1.0 - lb) * _sigmoid(-fp)

    ones_bd = ones_ref[...]
    tri = tri_ref[...]
    cum = jnp.concatenate([_mask_dot(tri, logf[b * tt:(b + 1) * tt]) for b in range(nbat)], axis=0)

    q3 = q.reshape(nsub, SUB, w)
    k3 = k.reshape(nsub, SUB, w)
    v3 = v.reshape(nsub, SUB, w)
    c3 = cum.reshape(nsub, SUB, w)
    lk3 = (cum - jnp.log(k)).reshape(nsub, SUB, w)
    row = lax.broadcasted_iota(jnp.int32, (nsub, SUB, w), 1)

    o_intra = jnp.zeros((nsub, SUB, w), F32)
    for j in range(SUB):
        keep = row >= j
        dec = jnp.where(keep, jnp.exp(c3 - lk3[:, j:j + 1, :]), 0.0)
        x = (q3 * dec).reshape(rows, w).astype(BF16)
        attn = _dot(x, ones_bd)
        o_intra = o_intra + attn.reshape(nsub, SUB, w) * v3[:, j:j + 1, :]

    c_last = c3[:, SUB - 1:SUB, :]
    qd = (q3 * jnp.exp(c3)).astype(BF16)
    kd = (k3 * jnp.exp(c_last - c3)).astype(BF16)
    v_bf = v3.astype(BF16)
    blk = (lax.broadcasted_iota(jnp.int32, (w, w), 0) // A_HEAD_DIM
           == lax.broadcasted_iota(jnp.int32, (w, w), 1) // A_HEAD_DIM)
    sts = [st_ref[b] for b in range(nbat)]
    o_parts = [[None] * nsub_b for _ in range(nbat)]
    for i in range(nsub_b):
        for b in range(nbat):
            n = b * nsub_b + i
            o_parts[b][i] = _dot_nt(qd[n], sts[b].astype(BF16)) + o_intra[n]
            upd = _dot_tn(v_bf[n], kd[n])
            sts[b] = sts[b] * jnp.exp(c_last[n]) + jnp.where(blk, upd, 0.0)
    for b in range(nbat):
        st_ref[b] = sts[b]
    o = jnp.concatenate([o_parts[b][i] for b in range(nbat) for i in range(nsub_b)], axis=0)

    ms = _dot_mask(o * o, ones_bd) * (1.0 / A_HEAD_DIM)
    o = o * lax.rsqrt(ms + EPS) * g_ref[...] * _silu(cols(3))
    o_ref[...] = o.reshape(nbat, tt, w).astype(o_ref.dtype)


def _hgrn2(pa, lb_logits, norm_g, layer, batch):
    t = pa.shape[0]
    s = t // batch
    tt = min(HGRN_TILE, s)
    nbat = min(MIX_BATCH, batch)
    w = A_WIDTH
    idx = np.arange(w)
    ones_bd = jnp.asarray(idx[:, None] // A_HEAD_DIM == idx[None, :] // A_HEAD_DIM, BF16)
    it = np.arange(tt)
    tri = jnp.asarray((it[:, None] // SUB == it[None, :] // SUB) & (it[:, None] >= it[None, :]), BF16)
    g = jnp.tile(norm_g.astype(F32), A_HEADS).reshape(1, w)
    tile = lambda b, i: (b, i, 0)
    const = lambda b, i: (0, 0)
    out = pl.pallas_call(
        functools.partial(_hgrn2_kernel, layer=layer),
        grid=(batch // nbat, s // tt),
        in_specs=[pl.BlockSpec((nbat, tt, PA_W), tile), pl.BlockSpec(lb_logits.shape, const),
                  pl.BlockSpec((1, w), const), pl.BlockSpec((w, w), const),
                  pl.BlockSpec((tt, tt), const)],
        out_specs=pl.BlockSpec((nbat, tt, w), tile),
        out_shape=jax.ShapeDtypeStruct((batch, s, w), BF16),
        scratch_shapes=[pltpu.VMEM((nbat, w, w), F32)],
        compiler_params=pltpu.CompilerParams(dimension_semantics=("arbitrary", "arbitrary"),
                                             vmem_limit_bytes=VMEM_LIMIT),
        name="hgrn2",
    )(pa.reshape(batch, s, PA_W), lb_logits.astype(F32), g, ones_bd, tri)
    return out.reshape(t, w)


def _gdn_kernel(pb_ref, pba_ref, cw_ref, alog_ref, dtb_ref, g_ref, tri_ref, o_ref,
                xbuf_ref, s_ref):
    nbat, tt = pb_ref.shape[0], pb_ref.shape[1]
    hd = B_HEAD_DIM
    rows = nbat * tt
    nchunk = tt // CHUNK
    nct = nbat * nchunk
    qkv_w = 3 * B_WIDTH

    @pl.when(pl.program_id(1) == 0)
    def _():
        s_ref[...] = jnp.zeros_like(s_ref)
        xbuf_ref[...] = jnp.zeros_like(xbuf_ref)

    qkv = _silu(jnp.concatenate(
        [_causal_conv(pb_ref[b, :, 0:qkv_w], xbuf_ref, b, cw_ref) for b in range(nbat)], axis=0))

    lane = lax.broadcasted_iota(jnp.int32, (1, LANES), 1)
    is_alpha = (lane >= B_HEADS) & (lane < 2 * B_HEADS)
    ba = pba_ref[...].reshape(rows, LANES)
    beta_all = _sigmoid(ba)
    rate = jnp.where(is_alpha, jnp.exp(alog_ref[...]), 0.0)
    log_alpha = -rate * _softplus(ba + dtb_ref[...])
    tri = tri_ref[...]
    gc_all = jnp.concatenate([_mask_dot(tri, log_alpha[b * tt:(b + 1) * tt]) for b in range(nbat)], axis=0)
    gc_t = gc_all.T

    r_i = lax.broadcasted_iota(jnp.int32, (CHUNK, CHUNK), 0)
    c_i = lax.broadcasted_iota(jnp.int32, (CHUNK, CHUNK), 1)
    causal = r_i >= c_i
    strict = r_i > c_i
    eye = (r_i == c_i).astype(F32)
    scale = hd ** -0.5

    nchain = B_HEADS * nbat

    def chunks(xs):
        x4 = [x.reshape(nbat, nchunk, CHUNK, x.shape[-1]) for x in xs]
        return jnp.concatenate([x4[h][:, c] for c in range(nchunk) for h in range(B_HEADS)], axis=0)

    q_l, k_l, v_l, beta_l, g_l, grow_l = [], [], [], [], [], []
    for h in range(B_HEADS):
        qh = qkv[:, h * hd:(h + 1) * hd]
        kh = qkv[:, B_WIDTH + h * hd:B_WIDTH + (h + 1) * hd]
        q_l.append(qh * lax.rsqrt(jnp.sum(qh * qh, axis=-1, keepdims=True) + EPS))
        k_l.append(kh * lax.rsqrt(jnp.sum(kh * kh, axis=-1, keepdims=True) + EPS))
        v_l.append(qkv[:, 2 * B_WIDTH + h * hd:2 * B_WIDTH + (h + 1) * hd])
        beta_l.append(jnp.broadcast_to(beta_all[:, h:h + 1], (rows, hd)))
        g_l.append(jnp.broadcast_to(gc_all[:, B_HEADS + h:B_HEADS + h + 1], (rows, hd)))
    for c in range(nchunk):
        for h in range(B_HEADS):
            for bb in range(nbat):
                r0 = (bb * nchunk + c) * CHUNK
                grow_l.append(gc_t[B_HEADS + h:B_HEADS + h + 1, r0:r0 + CHUNK][None])
    q3, k3, v3, beta3, g3 = chunks(q_l), chunks(k_l), chunks(v_l), chunks(beta_l), chunks(g_l)
    g_row3 = jnp.concatenate(grow_l, axis=0)
    eg3 = jnp.exp(g3)
    g_last3 = g3[:, CHUNK - 1:CHUNK, :]
    kb3 = k3 * beta3
    k_bf = k3.astype(BF16)
    decay = jnp.where(causal, jnp.exp(jnp.where(causal, g3[:, :, 0:CHUNK] - g_row3, 0.0)), 0.0)
    kq = _bdot_nt(jnp.concatenate([kb3, q3], axis=1).astype(BF16), k_bf)
    a_mat = jnp.where(strict, kq[:, 0:CHUNK] * decay, 0.0)
    qk = kq[:, CHUNK:2 * CHUNK] * scale * decay
    t_inv = eye - jnp.where(r_i // 2 == c_i // 2, a_mat, 0.0)
    b = 2
    while b < CHUNK:
        off = jnp.where((r_i // (2 * b) == c_i // (2 * b)) & (r_i // b != c_i // b), a_mat, 0.0)
        t_bf = t_inv.astype(BF16)
        t_inv = t_inv - _bdot(t_bf, _bdot(off.astype(BF16), t_bf).astype(BF16))
        b *= 2
    rhs = jnp.concatenate([v3 * beta3, kb3 * eg3], axis=2).astype(BF16)
    uw = _bdot(t_inv.astype(BF16), rhs)
    qd3 = q3 * (scale * eg3)
    kd3 = k3 * jnp.exp(g_last3 - g3)
    sdec3 = jnp.exp(g_last3)

    lhs1 = jnp.concatenate([uw[:, :, hd:2 * hd], qd3], axis=1).astype(BF16)
    lhs2 = jnp.concatenate([qk, jnp.swapaxes(kd3, 1, 2)], axis=1).astype(BF16)

    s = s_ref[...]
    o_c = []
    for c in range(nchunk):
        sl = slice(c * nchain, (c + 1) * nchain)
        m1 = _bdot(lhs1[sl], s.astype(BF16))
        v_new = uw[sl, :, 0:hd] - m1[:, 0:CHUNK]
        m2 = _bdot(lhs2[sl], v_new.astype(BF16))
        o_c.append(m1[:, CHUNK:2 * CHUNK] + m2[:, 0:CHUNK])
        s = s * sdec3[sl] + m2[:, CHUNK:CHUNK + hd]
    s_ref[...] = s
    for h in range(B_HEADS):
        for bb in range(nbat):
            o = jnp.concatenate([o_c[c][h * nbat + bb] for c in range(nchunk)], axis=0)
            z = pb_ref[bb, :, qkv_w + h * hd:qkv_w + (h + 1) * hd]
            o_ref[bb, :, h * hd:(h + 1) * hd] = (_rms_scale(o) * g_ref[...] * _silu(z)).astype(o_ref.dtype)


def _gdn(pb, pba, conv_w, a_log, dt_bias, norm_g, batch):
    t = pb.shape[0]
    s = t // batch
    tt = min(GDN_TILE, s)
    nbat = min(MIX_BATCH, batch)
    it = np.arange(tt)
    tri = jnp.asarray((it[:, None] // CHUNK == it[None, :] // CHUNK) & (it[:, None] >= it[None, :]), BF16)
    alog = jnp.zeros((1, LANES), F32).at[0, B_HEADS:2 * B_HEADS].set(a_log.astype(F32))
    dtb = jnp.zeros((1, LANES), F32).at[0, B_HEADS:2 * B_HEADS].set(dt_bias.astype(F32))
    g = norm_g.astype(F32).reshape(1, B_HEAD_DIM)
    tile = lambda b, i: (b, i, 0)
    const = lambda b, i: (0, 0)
    out = pl.pallas_call(
        _gdn_kernel,
        grid=(batch // nbat, s // tt),
        in_specs=[pl.BlockSpec((nbat, tt, PB_W), tile), pl.BlockSpec((nbat, tt, PBA_W), tile),
                  pl.BlockSpec(conv_w.shape, const), pl.BlockSpec((1, LANES), const),
                  pl.BlockSpec((1, LANES), const), pl.BlockSpec((1, B_HEAD_DIM), const),
                  pl.BlockSpec((tt, tt), const)],
        out_specs=pl.BlockSpec((nbat, tt, B_WIDTH), tile),
        out_shape=jax.ShapeDtypeStruct((batch, s, B_WIDTH), BF16),
        scratch_shapes=[pltpu.VMEM((nbat, SUBLANES, 3 * B_WIDTH), F32),
                        pltpu.VMEM((nbat * B_HEADS, B_HEAD_DIM, B_HEAD_DIM), F32)],
        compiler_params=pltpu.CompilerParams(dimension_semantics=("arbitrary", "arbitrary"),
                                             vmem_limit_bytes=VMEM_LIMIT),
        name="gdn",
    )(pb.reshape(batch, s, PB_W), pba.reshape(batch, s, PBA_W), conv_w.astype(F32), alog, dtb, g, tri)
    return out.reshape(t, B_WIDTH)


def _rglru_kernel(pc_ref, cw_ref, cb_ref, wa_ref, ba_ref, wx_ref, bx_ref, lam_ref, o_ref,
                  xbuf_ref, h_ref):
    nbat, tt = pc_ref.shape[0], pc_ref.shape[1]
    w = C_WIDTH

    @pl.when(pl.program_id(1) == 0)
    def _():
        h_ref[...] = jnp.zeros_like(h_ref)
        xbuf_ref[...] = jnp.zeros_like(xbuf_ref)

    xc = jnp.concatenate([_causal_conv(pc_ref[b, :, 0:w], xbuf_ref, b, cw_ref) for b in range(nbat)],
                         axis=0) + cb_ref[...]
    x_bf = xc.astype(BF16)
    r = _sigmoid(_dot(x_bf, wa_ref[...]) + ba_ref[...])
    gate_i = _sigmoid(_dot(x_bf, wx_ref[...]) + bx_ref[...])
    log_a = -RG_C * r * _softplus(-lam_ref[...])
    a_all = jnp.exp(log_a)
    row = lax.broadcasted_iota(jnp.int32, (tt, w), 0)
    mult = jnp.sqrt(jnp.maximum(-jnp.tanh(log_a) * (a_all * a_all + 1.0), EPS))
    b_all = mult * gate_i * xc
    first = row + pl.program_id(1) * tt == 0

    for s in range(nbat):
        a = a_all[s * tt:(s + 1) * tt]
        b = jnp.where(first, (gate_i * xc)[s * tt:(s + 1) * tt], b_all[s * tt:(s + 1) * tt])
        d = 1
        while d < tt:
            keep = row >= d
            a_prev = jnp.where(keep, pltpu.roll(a, d, axis=0), 1.0)
            b_prev = jnp.where(keep, pltpu.roll(b, d, axis=0), 0.0)
            b = a * b_prev + b
            a = a * a_prev
            d *= 2
        h = b + a * h_ref[s, 0:1, :]
        h_ref[s] = jnp.broadcast_to(h[tt - 1:tt, :], (SUBLANES, w))
        y = pc_ref[s, :, w:2 * w]
        gelu = 0.5 * y * (1.0 + jnp.tanh(0.7978845608028654 * (y + 0.044715 * (y * y * y))))
        o_ref[s] = (gelu * h).astype(o_ref.dtype)


def _block_diag(wb):
    n, d, _ = wb.shape
    out = jnp.zeros((n * d, n * d), wb.dtype)
    for i in range(n):
        out = out.at[i * d:(i + 1) * d, i * d:(i + 1) * d].set(wb[i])
    return out


def _rglru(pc, conv_w, conv_b, w_a, b_a, w_x, b_x, lam, batch):
    t = pc.shape[0]
    s = t // batch
    tt = min(MIX_TILE, s)
    nbat = min(MIX_BATCH, batch)
    w = C_WIDTH
    vec = lambda a: a.astype(F32).reshape(1, w)
    tile = lambda b, i: (b, i, 0)
    const = lambda b, i: (0, 0)
    vspec = pl.BlockSpec((1, w), const)
    mspec = pl.BlockSpec((w, w), const)
    out = pl.pallas_call(
        _rglru_kernel,
        grid=(batch // nbat, s // tt),
        in_specs=[pl.BlockSpec((nbat, tt, PC_W), tile), pl.BlockSpec(conv_w.shape, const), vspec,
                  mspec, vspec, mspec, vspec, vspec],
        out_specs=pl.BlockSpec((nbat, tt, w), tile),
        out_shape=jax.ShapeDtypeStruct((batch, s, w), BF16),
        scratch_shapes=[pltpu.VMEM((nbat, SUBLANES, w), F32), pltpu.VMEM((nbat, SUBLANES, w), F32)],
        compiler_params=pltpu.CompilerParams(dimension_semantics=("arbitrary", "arbitrary"),
                                             vmem_limit_bytes=VMEM_LIMIT),
        name="rglru",
    )(pc.reshape(batch, s, PC_W), conv_w.astype(F32), vec(conv_b), _block_diag(w_a).astype(BF16), vec(b_a),
      _block_diag(w_x).astype(BF16), vec(b_x), vec(lam))
    return out.reshape(t, w)


def _reorder_w_in(w):
    a_end = PA_W
    b_end = a_end + PB_W
    ba_end = b_end + 2 * B_HEADS
    pad = jnp.zeros((w.shape[0], PBA_W - 2 * B_HEADS), w.dtype)
    return jnp.concatenate([w[:, :b_end], w[:, ba_end:], w[:, b_end:ba_end], pad], axis=1).astype(BF16)


def kernel(x, norm1_g, w_in, hgrn_lb_logits, hgrn_norm_g, gdn_conv_w, gdn_a_log, gdn_dt_bias, gdn_norm_g, lru_conv_w, lru_conv_b, lru_w_a, lru_b_a, lru_w_x, lru_b_x, lru_lambda, w_out, norm2_g, w_up, w_down, final_norm_g):
    bsz, seq, d = x.shape
    depth = w_in.shape[0]
    xf = x.reshape(bsz * seq, d).astype(F32)
    gf = final_norm_g.astype(F32).reshape(1, d)
    for l in range(depth):
        pa, pb, pc, pba = _inproj(xf, norm1_g[l].astype(F32).reshape(1, d), _reorder_w_in(w_in[l]))
        oa = _hgrn2(pa, hgrn_lb_logits, hgrn_norm_g[l], l, bsz)
        ob = _gdn(pb, pba, gdn_conv_w[l], gdn_a_log[l], gdn_dt_bias[l], gdn_norm_g[l], bsz)
        oc = _rglru(pc, lru_conv_w[l], lru_conv_b[l], lru_w_a[l], lru_b_a[l], lru_w_x[l], lru_b_x[l],
                    lru_lambda[l], bsz)
        xf = _outmlp(xf, oa, ob, oc, w_out[l].astype(BF16), norm2_g[l].astype(F32).reshape(1, d),
                     w_up[l].astype(BF16), w_down[l].astype(BF16), gf, l == depth - 1)
    return xf.reshape(bsz, seq, d).astype(x.dtype)
```
